```python
import jax
import jax.numpy as jnp
from jax import lax
import numpy as np

D_MODEL = 1024
BATCH = 4
SEQ = 8192
DEPTH = 2

MEM_TOKENS = 256
M_HEADS = 4
M_WIDTH = D_MODEL // 2
M_DV = M_WIDTH // M_HEADS
M_DQK = M_DV // 2
M_QK_WIDTH = M_HEADS * M_DQK
M_CONV = 4
M_CHUNK = 64
F_HEADS = 8
F_WIDTH = D_MODEL // 2
F_DH = F_WIDTH // F_HEADS
F_BLOCK = 128
R_HEADS = 8
R_WIDTH = D_MODEL // 2
R_DH = R_WIDTH // R_HEADS
R_LORA_W = 64
R_LORA_A = 64
R_LORA_V = 32
R_LORA_G = 128
R_GN_EPS = 64e-5
N_BRANCH = 3
X_HEADS = 4
X_DH = D_MODEL // X_HEADS
D_FF_DENSE = 128 * ((8 * D_MODEL // 3 + 127) // 128)
N_EXPERTS = 8
TOP_K = 2
D_FF_EXPERT = 7 * D_MODEL // 2
DEEPNORM_ALPHA = (2.0 * DEPTH) ** 0.25
DEEPNORM_BETA = (8.0 * DEPTH) ** -0.25
LN_EPS = 1e-5
HEAD_NORM_EPS = 1e-6

kernel_name = 'hybrid_mlstm_fox_rwkv7_moe_deepnorm'


def _in_layout(with_vres):
    cols = [('m_qk', 2 * M_QK_WIDTH), ('m_v', M_WIDTH), ('m_o', M_WIDTH), ('m_i', M_HEADS), ('m_f', M_HEADS),
            ('f_q', F_WIDTH), ('f_k', F_WIDTH), ('f_v', F_WIDTH), ('f_f', F_HEADS),
            ('gate', N_BRANCH * D_MODEL),
            ('r_r', R_WIDTH), ('r_k', R_WIDTH), ('r_v', R_WIDTH), ('r_w', R_LORA_W), ('r_a', R_LORA_A), ('r_g', R_LORA_G)]
    if with_vres:
        cols.append(('r_vres', R_LORA_V))
    layout, start = {}, 0
    for name, width in cols:
        layout[name] = (start, start + width)
        start += width
    return layout, start


def _layer_norm(x, g, b):
    x32 = x.astype(jnp.float32)
    mu = x32.mean(-1, keepdims=True)
    var = jnp.square(x32 - mu).mean(-1, keepdims=True)
    return ((x32 - mu) * lax.rsqrt(var + LN_EPS)).astype(x.dtype) * g + b


def _post_norm(x, sub, g, b):
    return _layer_norm(DEEPNORM_ALPHA * x + sub, g, b)


def _head_standardise(h, eps):
    h32 = h.astype(jnp.float32)
    mu = h32.mean(-1, keepdims=True)
    var = jnp.square(h32 - mu).mean(-1, keepdims=True)
    return ((h32 - mu) * lax.rsqrt(var + eps)).reshape(h.shape[:2] + (-1,))


def _head_l2_normalise(u):
    B, S, C = u.shape
    uh = u.astype(jnp.float32).reshape(B, S, R_HEADS, R_DH)
    uh = uh / jnp.maximum(jnp.sqrt(jnp.sum(uh * uh, -1, keepdims=True)), 1e-12)
    return uh.reshape(B, S, C).astype(u.dtype)


def _shift(u):
    return jnp.pad(u[:, :-1], ((0, 0), (1, 0), (0, 0)))


def _causal_conv(u, w):
    K, C = w.shape
    return lax.conv_general_dilated(u, w[:, None, :].astype(u.dtype), window_strides=(1,), padding=((K - 1, 0),),
                                    dimension_numbers=('NWC', 'WIO', 'NWC'), feature_group_count=C)


def _mlstm(q, k, v, i_pre, f_pre):
    B, S, H, DK = q.shape
    L = M_CHUNK
    NC = S // L
    f32 = jnp.float32

    def chunks(a):
        a = a.astype(f32).reshape((B, NC, L, H) + a.shape[3:])
        return jnp.moveaxis(a, (1, 3), (0, 2))

    xs = (chunks(q), chunks(k) * DK ** -0.5, chunks(v), chunks(i_pre), chunks(jax.nn.log_sigmoid(f_pre.astype(f32))))
    tri = jnp.tril(jnp.ones((L, L), dtype=bool))

    def step(carry, xs_c):
        C, n, m = carry
        qc, kc, vc, ic, lfc = xs_c
        b = jnp.cumsum(lfc, axis=-1)
        d_intra = jnp.where(tri, b[..., :, None] - b[..., None, :] + ic[..., None, :], -jnp.inf)
        d_inter = b + m[..., None]
        m_t = jnp.maximum(d_inter, d_intra.max(-1))
        w_intra = jnp.einsum('bhtd,bhsd->bhts', qc, kc) * jnp.exp(d_intra - m_t[..., None])
        w_inter = jnp.exp(d_inter - m_t)
        num = w_inter[..., None] * jnp.einsum('bhtd,bhde->bhte', qc, C) + jnp.einsum('bhts,bhse->bhte', w_intra, vc)
        den = w_inter * jnp.einsum('bhtd,bhd->bht', qc, n) + w_intra.sum(-1)
        h = num / jnp.maximum(jnp.abs(den), jnp.exp(-m_t))[..., None]
        g = b[..., -1:] - b + ic
        m_new = jnp.maximum(b[..., -1] + m, g.max(-1))
        decay = jnp.exp(b[..., -1] + m - m_new)
        ws = jnp.exp(g - m_new[..., None])
        C = decay[..., None, None] * C + jnp.einsum('bhs,bhsd,bhse->bhde', ws, kc, vc)
        n = decay[..., None] * n + jnp.einsum('bhs,bhsd->bhd', ws, kc)
        return (C, n, m_new), h

    init = (jnp.zeros((B, H, DK, v.shape[-1]), f32), jnp.zeros((B, H, DK), f32), jnp.zeros((B, H), f32))
    _, h = lax.scan(step, init, xs)
    return jnp.moveaxis(h, (0, 2), (1, 3)).reshape(B, S, H, -1)


def _forgetting_attention(q, k, v, f_pre):
    B, S, H, DH = q.shape
    NB = S // F_BLOCK
    cum = jnp.cumsum(jax.nn.log_sigmoid(f_pre.astype(jnp.float32)), axis=1).transpose(0, 2, 1)
    qh = q.transpose(0, 2, 1, 3) * DH ** -0.5
    kh = k.transpose(0, 2, 1, 3)
    vh = v.transpose(0, 2, 1, 3)
    q_blocks = qh.reshape(B, H, NB, F_BLOCK, DH).transpose(2, 0, 1, 3, 4)
    cq_blocks = cum.reshape(B, H, NB, F_BLOCK).transpose(2, 0, 1, 3)
    starts = jnp.arange(NB, dtype=jnp.int32) * F_BLOCK
    k_pos = jnp.arange(S, dtype=jnp.int32)

    def block(args):
        qb, cqb, start = args
        logits = jnp.einsum('bhtd,bhsd->bhts', qb, kh).astype(jnp.float32) + cqb[..., None] - cum[:, :, None, :]
        q_pos = start + jnp.arange(F_BLOCK, dtype=jnp.int32)
        logits = jnp.where(k_pos[None, :] <= q_pos[:, None], logits, -jnp.inf)
        probs = jax.nn.softmax(logits, axis=-1).astype(vh.dtype)
        return jnp.einsum('bhts,bhsd->bhtd', probs, vh)

    o = lax.map(block, (q_blocks, cq_blocks, starts))
    return o.transpose(1, 0, 3, 2, 4).reshape(B, S, H * DH)


def _rwkv7_scan(r, decay, k, v, kk, kk_a):
    B, S, _ = r.shape

    def heads(t):
        return t.astype(jnp.float32).reshape(B, S, R_HEADS, R_DH).transpose(1, 0, 2, 3)

    xs = (heads(r), heads(decay), heads(k), heads(v), heads(kk), heads(kk_a))

    def step(state, xs_t):
        r_t, w_t, k_t, v_t, kk_t, b_t = xs_t
        sk = jnp.einsum('bhvk,bhk->bhv', state, kk_t)
        state = state * w_t[:, :, None, :] - sk[..., None] * b_t[:, :, None, :] + v_t[..., None] * k_t[:, :, None, :]
        return state, jnp.einsum('bhvk,bhk->bhv', state, r_t)

    _, y = lax.scan(step, jnp.zeros((B, R_HEADS, R_DH, R_DH), jnp.float32), xs)
    return y.transpose(1, 0, 2, 3)


def _token_mixer(x, w_in, b_in, m_conv, m_norm, m_up, f_up, r_mu, r_wbias, r_wB, r_abias, r_aB, r_gB,
                 r_kk, r_ka, r_rk, r_ln_g, r_ln_b, r_up, w_out, vres):
    B, S, _ = x.shape
    layout, _ = _in_layout(vres is not None)
    p = x @ w_in + b_in

    def col(name):
        s, e = layout[name]
        return p[..., s:e]

    qk = jax.nn.silu(_causal_conv(col('m_qk'), m_conv))
    m_q = qk[..., :M_QK_WIDTH].reshape(B, S, M_HEADS, M_DQK)
    m_k = qk[..., M_QK_WIDTH:].reshape(B, S, M_HEADS, M_DQK)
    m_v = col('m_v').reshape(B, S, M_HEADS, M_DV)
    h_a = _mlstm(m_q, m_k, m_v, col('m_i'), col('m_f'))
    h_a = _head_standardise(h_a, HEAD_NORM_EPS).astype(x.dtype) * m_norm * jax.nn.sigmoid(col('m_o'))

    h_b = _forgetting_attention(col('f_q').reshape(B, S, F_HEADS, F_DH), col('f_k').reshape(B, S, F_HEADS, F_DH),
                                col('f_v').reshape(B, S, F_HEADS, F_DH), col('f_f'))

    r0 = layout['r_r'][0]
    pr = p[..., r0:]
    pr = pr + r_mu * (_shift(pr) - pr)

    def rcol(name):
        s, e = layout[name]
        return pr[..., s - r0:e - r0]

    r, k, v = rcol('r_r'), rcol('r_k'), rcol('r_v')
    w_log = -jax.nn.softplus(-(r_wbias + jnp.tanh(rcol('r_w')) @ r_wB)) - 0.5
    decay = jnp.exp(-jnp.exp(w_log.astype(jnp.float32)))
    a = jax.nn.sigmoid(r_abias + rcol('r_a') @ r_aB)
    v_own = v
    if vres is not None:
        v_first, r_vbias, r_vB = vres
        v = v + (v_first - v) * jax.nn.sigmoid(r_vbias + rcol('r_vres') @ r_vB)
    g = jax.nn.sigmoid(rcol('r_g')) @ r_gB
    kk = _head_l2_normalise(k * r_kk)
    k = k * (1.0 + (a - 1.0) * r_ka)
    y = _rwkv7_scan(r, decay, k, v, kk, kk * a)
    y = _head_standardise(y, R_GN_EPS).astype(x.dtype) * r_ln_g + r_ln_b
    bonus = jnp.sum((r * k * r_rk).reshape(B, S, R_HEADS, R_DH), axis=-1, keepdims=True)
    y = (y + (bonus * v.reshape(B, S, R_HEADS, R_DH)).reshape(B, S, R_WIDTH)) * g

    gates = jax.nn.sigmoid(col('gate')).reshape(B, S, N_BRANCH, D_MODEL)
    merged = gates[..., 0, :] * (h_a @ m_up) + gates[..., 1, :] * (h_b @ f_up) + gates[..., 2, :] * (y @ r_up)
    return merged @ w_out, v_own


def _cross_attention(x, mem_n, wq, wkv, wo):
    B, S, _ = x.shape
    M = mem_n.shape[1]
    q = (x @ wq).reshape(B, S, X_HEADS, X_DH)
    kv = (mem_n @ wkv).reshape(B, M, 2, X_HEADS, X_DH)
    logits = jnp.einsum('bthd,bmhd->bhtm', q, kv[:, :, 0]).astype(jnp.float32) * X_DH ** -0.5
    probs = jax.nn.softmax(logits, axis=-1).astype(x.dtype)
    o = jnp.einsum('bhtm,bmhd->bthd', probs, kv[:, :, 1]).reshape(B, S, D_MODEL)
    return o @ wo


def _swiglu(x, wgu, wd):
    gate, up = jnp.split(x @ wgu, 2, axis=-1)
    return (jax.nn.silu(gate) * up) @ wd


def _moe(x, router_w, router_b, ex_wgu, ex_wd):
    logits = (x @ router_w).astype(jnp.float32) + router_b
    top_v, top_i = lax.top_k(logits, TOP_K)
    top_w = jax.nn.softmax(top_v, axis=-1)
    comb = jnp.sum(jax.nn.one_hot(top_i, N_EXPERTS, dtype=jnp.float32) * top_w[..., None], axis=-2).astype(x.dtype)
    y = jnp.zeros_like(x)
    for e in range(N_EXPERTS):
        y = y + comb[..., e:e + 1] * _swiglu(x, ex_wgu[e], ex_wd[e])
    return y


def setup_inputs(seed: int = 0) -> dict:
    key = jax.random.key(seed)
    keys = iter(jax.random.split(key, 128))

    def nrm(shape, scale):
        return scale * jax.random.normal(next(keys), shape, jnp.float32)

    def unif(shape, lo, hi):
        return jax.random.uniform(next(keys), shape, jnp.float32, lo, hi)

    def gain(n):
        return 1.0 + nrm((n,), 0.02)

    inp = {}
    inp['x'] = nrm((BATCH, SEQ, D_MODEL), 1.0)
    inp['mem'] = nrm((BATCH, MEM_TOKENS, D_MODEL), 1.0)
    inp['mem_ln_g'] = gain(D_MODEL)
    inp['mem_ln_b'] = nrm((D_MODEL,), 0.02)
    for l in range(DEPTH):
        s = f'_{l}'
        layout, n_cols = _in_layout(l > 0)
        inp['w_in' + s] = nrm((D_MODEL, n_cols), D_MODEL ** -0.5)
        b_in = nrm((n_cols,), 0.02)
        b_in = b_in.at[layout['m_f'][0]:layout['m_f'][1]].add(jnp.linspace(3.0, 6.0, M_HEADS))
        b_in = b_in.at[layout['f_f'][0]:layout['f_f'][1]].add(jnp.linspace(1.0, 6.0, F_HEADS))
        inp['b_in' + s] = b_in
        inp['m_conv' + s] = nrm((M_CONV, 2 * M_QK_WIDTH), M_CONV ** -0.5)
        inp['m_norm' + s] = gain(M_WIDTH)
        inp['m_up' + s] = nrm((M_WIDTH, D_MODEL), M_WIDTH ** -0.5)
        inp['f_up' + s] = nrm((F_WIDTH, D_MODEL), F_WIDTH ** -0.5)
        inp['r_mu' + s] = unif((n_cols - layout['r_r'][0],), 0.0, 1.0)
        inp['r_wbias' + s] = unif((R_WIDTH,), -5.0, -1.0)
        inp['r_wB' + s] = nrm((R_LORA_W, R_WIDTH), R_LORA_W ** -0.5)
        inp['r_abias' + s] = nrm((R_WIDTH,), 0.1)
        inp['r_aB' + s] = nrm((R_LORA_A, R_WIDTH), R_LORA_A ** -0.5)
        if l > 0:
            inp['r_vbias' + s] = nrm((R_WIDTH,), 0.1)
            inp['r_vB' + s] = nrm((R_LORA_V, R_WIDTH), R_LORA_V ** -0.5)
        inp['r_gB' + s] = nrm((R_LORA_G, R_WIDTH), R_LORA_G ** -0.5)
        inp['r_kk' + s] = 0.85 + nrm((R_WIDTH,), 0.02)
        inp['r_ka' + s] = 1.0 + nrm((R_WIDTH,), 0.02)
        inp['r_rk' + s] = nrm((R_WIDTH,), 0.1)
        inp['r_ln_g' + s] = gain(R_WIDTH)
        inp['r_ln_b' + s] = nrm((R_WIDTH,), 0.02)
        inp['r_up' + s] = nrm((R_WIDTH, D_MODEL), R_WIDTH ** -0.5)
        inp['w_out' + s] = nrm((D_MODEL, D_MODEL), DEEPNORM_BETA * D_MODEL ** -0.5)
        inp['ln1_g' + s] = gain(D_MODEL)
        inp['ln1_b' + s] = nrm((D_MODEL,), 0.02)
        inp['x_wq' + s] = nrm((D_MODEL, D_MODEL), D_MODEL ** -0.5)
        inp['x_wkv' + s] = nrm((D_MODEL, 2 * D_MODEL), D_MODEL ** -0.5)
        inp['x_wo' + s] = nrm((D_MODEL, D_MODEL), DEEPNORM_BETA * D_MODEL ** -0.5)
        inp['ln2_g' + s] = gain(D_MODEL)
        inp['ln2_b' + s] = nrm((D_MODEL,), 0.02)
        if l % 2 == 0:
            inp['ff_wgu' + s] = nrm((D_MODEL, 2 * D_FF_DENSE), D_MODEL ** -0.5)
            inp['ff_wd' + s] = nrm((D_FF_DENSE, D_MODEL), DEEPNORM_BETA * D_FF_DENSE ** -0.5)
        else:
            inp['ex_router' + s] = nrm((D_MODEL, N_EXPERTS), D_MODEL ** -0.5)
            inp['ex_router_b' + s] = nrm((N_EXPERTS,), 0.01)
            inp['ex_wgu' + s] = nrm((N_EXPERTS, D_MODEL, 2 * D_FF_EXPERT), D_MODEL ** -0.5)
            inp['ex_wd' + s] = nrm((N_EXPERTS, D_FF_EXPERT, D_MODEL), DEEPNORM_BETA * D_FF_EXPERT ** -0.5)
        inp['ln3_g' + s] = gain(D_MODEL)
        inp['ln3_b' + s] = nrm((D_MODEL,), 0.02)
    return inp


def reference(x, mem, mem_ln_g, mem_ln_b,
              w_in_0, b_in_0, m_conv_0, m_norm_0, m_up_0, f_up_0, r_mu_0, r_wbias_0, r_wB_0, r_abias_0, r_aB_0,
              r_gB_0, r_kk_0, r_ka_0, r_rk_0, r_ln_g_0, r_ln_b_0, r_up_0, w_out_0, ln1_g_0, ln1_b_0,
              x_wq_0, x_wkv_0, x_wo_0, ln2_g_0, ln2_b_0, ff_wgu_0, ff_wd_0, ln3_g_0, ln3_b_0,
              w_in_1, b_in_1, m_conv_1, m_norm_1, m_up_1, f_up_1, r_mu_1, r_wbias_1, r_wB_1, r_abias_1, r_aB_1,
              r_vbias_1, r_vB_1,
              r_gB_1, r_kk_1, r_ka_1, r_rk_1, r_ln_g_1, r_ln_b_1, r_up_1, w_out_1, ln1_g_1, ln1_b_1,
              x_wq_1, x_wkv_1, x_wo_1, ln2_g_1, ln2_b_1, ex_router_1, ex_router_b_1, ex_wgu_1, ex_wd_1,
              ln3_g_1, ln3_b_1):
    mem_n = _layer_norm(mem, mem_ln_g, mem_ln_b)
    mixer_params = (
        (w_in_0, b_in_0, m_conv_0, m_norm_0, m_up_0, f_up_0, r_mu_0, r_wbias_0, r_wB_0, r_abias_0, r_aB_0,
         r_gB_0, r_kk_0, r_ka_0, r_rk_0, r_ln_g_0, r_ln_b_0, r_up_0, w_out_0),
        (w_in_1, b_in_1, m_conv_1, m_norm_1, m_up_1, f_up_1, r_mu_1, r_wbias_1, r_wB_1, r_abias_1, r_aB_1,
         r_gB_1, r_kk_1, r_ka_1, r_rk_1, r_ln_g_1, r_ln_b_1, r_up_1, w_out_1),
    )
    vres_params = (None, (r_vbias_1, r_vB_1))
    ln1 = ((ln1_g_0, ln1_b_0), (ln1_g_1, ln1_b_1))
    xattn = ((x_wq_0, x_wkv_0, x_wo_0), (x_wq_1, x_wkv_1, x_wo_1))
    ln2 = ((ln2_g_0, ln2_b_0), (ln2_g_1, ln2_b_1))
    ffn = ((ff_wgu_0, ff_wd_0), (ex_router_1, ex_router_b_1, ex_wgu_1, ex_wd_1))
    ln3 = ((ln3_g_0, ln3_b_0), (ln3_g_1, ln3_b_1))

    v_first = None
    for l in range(DEPTH):
        vres = None if l == 0 else (v_first, *vres_params[l])
        mix, v_own = _token_mixer(x, *mixer_params[l], vres)
        if l == 0:
            v_first = v_own
        x = _post_norm(x, mix, *ln1[l])
        x = _post_norm(x, _cross_attention(x, mem_n, *xattn[l]), *ln2[l])
        f = _swiglu(x, *ffn[l]) if l % 2 == 0 else _moe(x, *ffn[l])
        x = _post_norm(x, f, *ln3[l])
    return x
```

```python
import functools

import jax
import jax.numpy as jnp
from jax import lax
from jax.experimental import pallas as pl
from jax.experimental.pallas import tpu as pltpu

F32 = jnp.float32
BF16 = jnp.bfloat16

D_MODEL = 1024
DEPTH = 2
M_HEADS, M_DV, M_DQK, M_CONV, M_CHUNK = 4, 128, 64, 4, 64
F_HEADS, F_DH = 8, 64
R_HEADS, R_DH, R_CHUNK = 8, 64, 64
R_GN_EPS = 64e-5
X_HEADS, X_DH = 4, 256
N_EXPERTS = 8
ALPHA = (2.0 * DEPTH) ** 0.25
LN_EPS = 1e-5
HEAD_NORM_EPS = 1e-6
NEG = -1e30

C_MQK, C_MV, C_MO, C_FQ, C_FK, C_FV, C_GATE, C_RR = 0, 512, 1024, 1536, 2048, 2560, 3072, 6144
C_SMALL = 7936
NP = 8192
VMEM_LIMIT = 56 * 1024 * 1024


def _cp(*sem):
    return pltpu.CompilerParams(dimension_semantics=sem, vmem_limit_bytes=VMEM_LIMIT)


def _dot(a, b):
    return jnp.dot(a, b, preferred_element_type=F32)


def _dot_nt(a, b):
    return lax.dot_general(a, b, (((1,), (1,)), ((), ())), preferred_element_type=F32)


def _dot_tn(a, b):
    return lax.dot_general(a, b, (((0,), (0,)), ((), ())), preferred_element_type=F32)


def _split(x):
    hi = x.astype(BF16)
    lo = (x - hi.astype(F32)).astype(BF16)
    return hi, lo


def _dot_x2(a, b16):
    hi, lo = _split(a)
    return _dot(hi, b16) + _dot(lo, b16)


def _dot_x3(a, b):
    ah, al = _split(a)
    bh, bl = _split(b)
    return _dot(ah, bh) + (_dot(al, bh) + _dot(ah, bl))


def _sigmoid(x):
    return 1.0 / (1.0 + jnp.exp(-x))


def _log_sigmoid(x):
    return jnp.minimum(x, 0.0) - jnp.log(1.0 + jnp.exp(-jnp.abs(x)))


def _softplus(x):
    return jnp.maximum(x, 0.0) + jnp.log(1.0 + jnp.exp(-jnp.abs(x)))


def _silu(x):
    return x * _sigmoid(x)


def _ln(z, g, b, eps=LN_EPS):
    mu = jnp.mean(z, axis=-1, keepdims=True)
    zc = z - mu
    var = jnp.mean(zc * zc, axis=-1, keepdims=True)
    return zc * lax.rsqrt(var + eps) * g + b


def _mm_kernel(a_ref, w_ref, b_ref, o_ref):
    o_ref[...] = (_dot(a_ref[...], w_ref[...]) + b_ref[...]).astype(o_ref.dtype)


def _matmul_bias(a, w, b, out_dtype, tm, tn, name):
    m, k = a.shape
    n = w.shape[1]
    tm, tn = min(tm, m), min(tn, n)
    return pl.pallas_call(
        _mm_kernel,
        grid=(m // tm, n // tn),
        in_specs=[pl.BlockSpec((tm, k), lambda i, j: (i, 0)),
                  pl.BlockSpec((k, tn), lambda i, j: (0, j)),
                  pl.BlockSpec((1, tn), lambda i, j: (0, j))],
        out_specs=pl.BlockSpec((tm, tn), lambda i, j: (i, j)),
        out_shape=jax.ShapeDtypeStruct((m, n), out_dtype),
        compiler_params=_cp("parallel", "parallel"),
        name=name,
    )(a, w, b)


def _memkv_kernel(mem_ref, g_ref, b_ref, w0_ref, w1_ref, o0_ref, o1_ref):
    mn = _ln(mem_ref[0], g_ref[...], b_ref[...]).astype(BF16)
    o0_ref[0] = _dot(mn, w0_ref[...]).astype(BF16)
    o1_ref[0] = _dot(mn, w1_ref[...]).astype(BF16)


def _mem_kv(mem, g, b, wkv0, wkv1):
    bsz, m, d = mem.shape
    n = wkv0.shape[1]
    full = lambda i: (0, 0)
    return pl.pallas_call(
        _memkv_kernel,
        grid=(bsz,),
        in_specs=[pl.BlockSpec((1, m, d), lambda i: (i, 0, 0)),
                  pl.BlockSpec((1, d), full), pl.BlockSpec((1, d), full),
                  pl.BlockSpec((d, n), full), pl.BlockSpec((d, n), full)],
        out_specs=[pl.BlockSpec((1, m, n), lambda i: (i, 0, 0))] * 2,
        out_shape=[jax.ShapeDtypeStruct((bsz, m, n), BF16)] * 2,
        compiler_params=_cp("parallel"),
        name="mem_kv",
    )(mem, g, b, wkv0, wkv1)


def _cum_kernel(g_ref, o_ref, *, blk):
    s = g_ref.shape[2]
    r = lax.broadcasted_iota(jnp.int32, (blk, blk), 0)
    c = lax.broadcasted_iota(jnp.int32, (blk, blk), 1)
    upper = (r <= c).astype(BF16)

    def body(i, carry):
        off = pl.multiple_of(i * blk, blk)
        ls = _log_sigmoid(g_ref[0, :, pl.ds(off, blk)])
        h1 = ls.astype(BF16)
        r1 = ls - h1.astype(F32)
        h2 = r1.astype(BF16)
        h3 = (r1 - h2.astype(F32)).astype(BF16)
        cs = _dot(h1, upper) + (_dot(h2, upper) + _dot(h3, upper)) + carry
        o_ref[0, :, pl.ds(off, blk)] = cs
        return cs[:, blk - 1:blk]

    lax.fori_loop(0, s // blk, body, jnp.zeros((8, 1), F32))


def _fox_cumsum(gates_t):
    bsz, _, s = gates_t.shape
    return pl.pallas_call(
        functools.partial(_cum_kernel, blk=128),
        grid=(bsz,),
        in_specs=[pl.BlockSpec((1, 8, s), lambda b: (b, 1, 0))],
        out_specs=pl.BlockSpec((1, 8, s), lambda b: (b, 0, 0)),
        out_shape=jax.ShapeDtypeStruct((bsz, 8, s), F32),
        compiler_params=_cp("parallel"),
        name="fox_cumsum",
    )(gates_t)


def _fox_kernel(q_ref, k_ref, v_ref, cq_ref, ck_ref, o_ref, m_sc, l_sc, acc_sc, *, tq, tk):
    qi = pl.program_id(1)
    ki = pl.program_id(2)

    @pl.when(ki == 0)
    def _():
        m_sc[...] = jnp.full(m_sc.shape, NEG, F32)
        l_sc[...] = jnp.zeros(l_sc.shape, F32)
        acc_sc[...] = jnp.zeros(acc_sc.shape, F32)

    @pl.when(ki <= qi)
    def _():
        row = qi * tq + lax.broadcasted_iota(jnp.int32, (tq, tk), 0)
        col = ki * tk + lax.broadcasted_iota(jnp.int32, (tq, tk), 1)
        mask = col <= row
        cq = cq_ref[0]
        ck = ck_ref[0]
        for h in range(F_HEADS):
            sl = slice(h * F_DH, (h + 1) * F_DH)
            s = _dot_nt(q_ref[0, :, sl], k_ref[0, :, sl]) * (F_DH ** -0.5)
            lg = s + (cq[:, h:h + 1] - ck[h:h + 1, :])
            lg = jnp.where(mask, lg, NEG)
            m_old = m_sc[h]
            m_new = jnp.maximum(m_old, jnp.max(lg, axis=1, keepdims=True))
            a = jnp.exp(m_old - m_new)
            p = jnp.exp(lg - m_new)
            l_sc[h] = a * l_sc[h] + jnp.sum(p, axis=1, keepdims=True)
            acc_sc[:, sl] = a * acc_sc[:, sl] + _dot(p.astype(BF16), v_ref[0, :, sl])
            m_sc[h] = m_new

    @pl.when(ki == qi)
    def _():
        for h in range(F_HEADS):
            sl = slice(h * F_DH, (h + 1) * F_DH)
            o_ref[0, :, sl] = (acc_sc[:, sl] / l_sc[h]).astype(o_ref.dtype)


def _fox_attention(p16, cum, cumcol, t):
    bsz, s, _ = p16.shape
    t = min(t, s)
    n = s // t
    w = F_HEADS * F_DH
    kmap = lambda c: (lambda b, i, j: (b, jnp.minimum(i, j), c))
    return pl.pallas_call(
        functools.partial(_fox_kernel, tq=t, tk=t),
        grid=(bsz, n, n),
        in_specs=[pl.BlockSpec((1, t, w), lambda b, i, j: (b, i, C_FQ // w)),
                  pl.BlockSpec((1, t, w), kmap(C_FK // w)),
                  pl.BlockSpec((1, t, w), kmap(C_FV // w)),
                  pl.BlockSpec((1, t, 8), lambda b, i, j: (b, i, 0)),
                  pl.BlockSpec((1, 8, t), lambda b, i, j: (b, 0, jnp.minimum(i, j)))],
        out_specs=pl.BlockSpec((1, t, w), lambda b, i, j: (b, i, 0)),
        out_shape=jax.ShapeDtypeStruct((bsz, s, w), BF16),
        scratch_shapes=[pltpu.VMEM((F_HEADS, t, 1), F32), pltpu.VMEM((F_HEADS, t, 1), F32),
                        pltpu.VMEM((t, w), F32)],
        compiler_params=_cp("parallel", "parallel", "arbitrary"),
        name="fox_attention",
    )(p16, p16, p16, cumcol, cum)


def _mlstm_kernel(qk_ref, v_ref, o_ref, g_ref, gt_ref, cw_ref, nw_ref, out_ref,
                  prev_sc, q_sc, k_sc, c_sc, m_sc, *, lt):
    ti = pl.program_id(1)
    L = M_CHUNK

    @pl.when(ti == 0)
    def _():
        prev_sc[...] = jnp.zeros(prev_sc.shape, F32)
        c_sc[...] = jnp.zeros(c_sc.shape, F32)
        m_sc[...] = jnp.zeros(m_sc.shape, F32)

    u = qk_ref[0].astype(F32)
    prev = prev_sc[...]
    rows = lax.broadcasted_iota(jnp.int32, u.shape, 0)
    acc = u * cw_ref[M_CONV - 1:M_CONV, :]
    for d in range(1, M_CONV):
        sh = jnp.where(rows < d, pltpu.roll(prev, d, 0), pltpu.roll(u, d, 0))
        acc = acc + sh * cw_ref[M_CONV - 1 - d:M_CONV - d, :]
    prev_sc[...] = u
    qk = _silu(acc)
    nq = M_HEADS * M_DQK
    q_sc[...] = qk[:, :nq]
    k_sc[...] = qk[:, nq:] * (M_DQK ** -0.5)

    r = lax.broadcasted_iota(jnp.int32, (L, L), 0)
    c = lax.broadcasted_iota(jnp.int32, (L, L), 1)
    tri = c <= r
    lower = tri.astype(F32)
    upper = (r <= c).astype(F32)
    ones = jnp.ones((L, M_DV), F32)

    for ch in range(lt // L):
        rs = slice(ch * L, (ch + 1) * L)
        gc = g_ref[0, rs, :]
        gr = gt_ref[0, :, rs]
        bcol = _dot_x3(lower, _log_sigmoid(gc))
        brow = _dot_x3(_log_sigmoid(gr), upper)
        for h in range(M_HEADS):
            bc = bcol[:, 4 + h:5 + h]
            br = brow[4 + h:5 + h, :]
            ic = gc[:, h:h + 1]
            ir = gr[h:h + 1, :]
            m_prev = m_sc[h:h + 1, 0:1]
            qh = q_sc[rs, h * M_DQK:(h + 1) * M_DQK].astype(BF16)
            kh = k_sc[rs, h * M_DQK:(h + 1) * M_DQK]
            vext = jnp.concatenate([v_ref[0, rs, h * M_DV:(h + 1) * M_DV].astype(F32), ones], axis=1).astype(BF16)
            cext = c_sc[h]
            dm = jnp.where(tri, bc - br + ir, -jnp.inf)
            d_inter = bc + m_prev
            m_t = jnp.maximum(d_inter, jnp.max(dm, axis=1, keepdims=True))
            w_intra = _dot_nt(qh, kh.astype(BF16)) * jnp.exp(dm - m_t)
            w_inter = jnp.exp(d_inter - m_t)
            ext = w_inter * _dot(qh, cext.astype(BF16)) + _dot(w_intra.astype(BF16), vext)
            num = ext[:, :M_DV]
            den = ext[:, M_DV:M_DV + 1]
            hh = num / jnp.maximum(jnp.abs(den), jnp.exp(-m_t))
            b_last = bc[L - 1:L, :]
            g = b_last - bc + ic
            m_new = jnp.maximum(b_last + m_prev, jnp.max(g, axis=0, keepdims=True))
            decay = jnp.exp(b_last + m_prev - m_new)
            ws = jnp.exp(g - m_new)
            c_sc[h] = decay * cext + _dot_tn((ws * kh).astype(BF16), vext)
            m_sc[h:h + 1, :] = jnp.broadcast_to(m_new, (1, m_sc.shape[1]))
            mu = jnp.mean(hh, axis=1, keepdims=True)
            hc = hh - mu
            var = jnp.mean(hc * hc, axis=1, keepdims=True)
            hs = slice(h * M_DV, (h + 1) * M_DV)
            hn = hc * lax.rsqrt(var + HEAD_NORM_EPS) * nw_ref[:, hs]
            out_ref[0, rs, hs] = (hn * _sigmoid(o_ref[0, rs, hs].astype(F32))).astype(out_ref.dtype)


def _mlstm(p16, gates, gates_t, conv_w, norm_w, lt):
    bsz, s, _ = p16.shape
    lt = min(lt, s)
    w = M_HEADS * M_DV
    full = lambda b, i: (0, 0)
    return pl.pallas_call(
        functools.partial(_mlstm_kernel, lt=lt),
        grid=(bsz, s // lt),
        in_specs=[pl.BlockSpec((1, lt, w), lambda b, i: (b, i, C_MQK // w)),
                  pl.BlockSpec((1, lt, w), lambda b, i: (b, i, C_MV // w)),
                  pl.BlockSpec((1, lt, w), lambda b, i: (b, i, C_MO // w)),
                  pl.BlockSpec((1, lt, 128), lambda b, i: (b, i, 0)),
                  pl.BlockSpec((1, 8, lt), lambda b, i: (b, 0, i)),
                  pl.BlockSpec((M_CONV, w), full),
                  pl.BlockSpec((1, w), full)],
        out_specs=pl.BlockSpec((1, lt, w), lambda b, i: (b, i, 0)),
        out_shape=jax.ShapeDtypeStruct((bsz, s, w), BF16),
        scratch_shapes=[pltpu.VMEM((lt, w), F32),
                        pltpu.VMEM((lt, M_HEADS * M_DQK), F32), pltpu.VMEM((lt, M_HEADS * M_DQK), F32),
                        pltpu.VMEM((M_HEADS, M_DQK, 2 * M_DV), F32), pltpu.VMEM((8, 128), F32)],
        compiler_params=_cp("parallel", "arbitrary"),
        name="mlstm",
    )(p16, p16, p16, gates, gates_t, conv_w, norm_w)


def _rprep_kernel(*refs, tt, has_vres):
    if has_vres:
        (p_ref, vf_ref, mu_ref, wa_ref, wv_ref, wg_ref, hs_ref, wbias_ref, abias_ref, vbias_ref,
         kkw_ref, kaw_ref, rkw_ref,
         r_o, lw_o, k_o, v_o, kk_o, b_o, g_o, bv_o, carry_sc) = refs
    else:
        (p_ref, mu_ref, wa_ref, wg_ref, hs_ref, wbias_ref, abias_ref,
         kkw_ref, kaw_ref, rkw_ref,
         r_o, lw_o, k_o, v_o, kk_o, b_o, g_o, bv_o, vown_o, carry_sc) = refs
    ti = pl.program_id(1)

    @pl.when(ti == 0)
    def _():
        carry_sc[...] = jnp.zeros(carry_sc.shape, F32)

    rows = lax.broadcasted_iota(jnp.int32, (tt, 512), 0)

    def shifted(c0, width):
        x = p_ref[0, :, c0:c0 + width].astype(F32)
        prev = jnp.where(rows[:, :width] == 0, carry_sc[0:1, c0:c0 + width], pltpu.roll(x, 1, 0))
        mu = mu_ref[:, c0:c0 + width]
        return x + mu * (prev - x)

    r = shifted(0, 512)
    k = shifted(512, 512)
    v = shifted(1024, 512)
    wa_in = shifted(1536, 128)
    g_in = shifted(1664, 128)
    lane = lax.broadcasted_iota(jnp.int32, (tt, 128), 1)
    wa_act = jnp.where(lane < 64, jnp.tanh(wa_in), wa_in).astype(BF16)
    wa = _dot(wa_act, wa_ref[...])
    w_log = -_softplus(-(wbias_ref[...] + wa[:, :512])) - 0.5
    lw_o[0] = -jnp.exp(w_log)
    a = _sigmoid(abias_ref[...] + wa[:, 512:])
    if has_vres:
        s_in = shifted(1792, 128)
        mix = _sigmoid(vbias_ref[...] + _dot(s_in.astype(BF16), wv_ref[...]))
        v = v + (vf_ref[0].astype(F32) - v) * mix
    else:
        vown_o[0] = v.astype(vown_o.dtype)
    g_o[0] = _dot(_sigmoid(g_in).astype(BF16), wg_ref[...]).astype(g_o.dtype)
    hs = hs_ref[...]
    u = k * kkw_ref[...]
    nrm = jnp.sqrt(_dot_x2(u * u, hs))
    kk = u / jnp.maximum(nrm, 1e-12)
    k2 = k * (1.0 + (a - 1.0) * kaw_ref[...])
    bonus = _dot_x2(r * k2 * rkw_ref[...], hs)
    r_o[0] = r.astype(r_o.dtype)
    k_o[0] = k2.astype(k_o.dtype)
    v_o[0] = v.astype(v_o.dtype)
    kk_o[0] = kk.astype(kk_o.dtype)
    b_o[0] = (kk * a).astype(b_o.dtype)
    bv_o[0] = (bonus * v).astype(bv_o.dtype)
    carry_sc[0:1, :] = p_ref[0, tt - 1:tt, :].astype(F32)


def _rwkv_prep(p16, v_first, mu, w_wa, w_v, w_g, headsum, wbias, abias, vbias, kkw, kaw, rkw, tt):
    bsz, s, _ = p16.shape
    tt = min(tt, s)
    has_vres = v_first is not None
    full = lambda b, i: (0, 0)
    tok = lambda b, i: (b, i, 0)
    vec = pl.BlockSpec((1, 512), full)
    ins = [p16]
    specs = [pl.BlockSpec((1, tt, 2048), lambda b, i: (b, i, C_RR // 2048))]
    if has_vres:
        ins.append(v_first)
        specs.append(pl.BlockSpec((1, tt, 512), tok))
    ins += [mu, w_wa]
    specs += [pl.BlockSpec((1, 2048), full), pl.BlockSpec((128, 1024), full)]
    if has_vres:
        ins.append(w_v)
        specs.append(pl.BlockSpec((128, 512), full))
    ins += [w_g, headsum, wbias, abias]
    specs += [pl.BlockSpec((128, 512), full), pl.BlockSpec((512, 512), full), vec, vec]
    if has_vres:
        ins.append(vbias)
        specs.append(vec)
    ins += [kkw, kaw, rkw]
    specs += [vec, vec, vec]
    n_out = 8 if has_vres else 9
    dts = [BF16, F32, BF16, BF16, BF16, BF16, BF16, BF16] + ([] if has_vres else [BF16])
    return pl.pallas_call(
        functools.partial(_rprep_kernel, tt=tt, has_vres=has_vres),
        grid=(bsz, s // tt),
        in_specs=specs,
        out_specs=[pl.BlockSpec((1, tt, 512), tok)] * n_out,
        out_shape=[jax.ShapeDtypeStruct((bsz, s, 512), dt) for dt in dts],
        scratch_shapes=[pltpu.VMEM((8, 2048), F32)],
        compiler_params=_cp("parallel", "arbitrary"),
        name="rwkv_prep",
    )(*ins)


def _rchunk_kernel(r_ref, lw_ref, k_ref, v_ref, kk_ref, b_ref, m_o, n_o, rq_o, y0_o):
    L = R_CHUNK
    ri = lax.broadcasted_iota(jnp.int32, (L, L), 0)
    ci = lax.broadcasted_iota(jnp.int32, (L, L), 1)
    lower = (ci <= ri).astype(F32)
    strict = ci < ri
    incl = ci <= ri
    eye = (ci == ri).astype(F32)

    lw = lw_ref[0]
    cl = _dot_x3(lower, lw)
    cl_last = cl[L - 1:L, :]
    w_in = jnp.exp(cl)
    w_ex = jnp.exp(cl - lw)
    w_inv = jnp.exp(-cl)
    w_rem = jnp.exp(cl_last - cl)
    w_last = jnp.exp(cl_last)
    kf = k_ref[0].astype(F32)
    bf = b_ref[0].astype(F32)
    kk_t = kk_ref[0].astype(F32) * w_ex
    r_t = r_ref[0].astype(F32) * w_in
    b_h = bf * w_inv
    k_h = kf * w_inv
    b_w = bf * w_rem
    k_w = kf * w_rem
    vf = v_ref[0]

    for h in range(R_HEADS):
        sl = slice(h * R_DH, (h + 1) * R_DH)
        lhs = jnp.concatenate([kk_t[:, sl], r_t[:, sl]], axis=0).astype(BF16)
        rhs = jnp.concatenate([b_h[:, sl], k_h[:, sl]], axis=0).astype(BF16)
        a = _dot_nt(lhs, rhs)
        a_ab = jnp.where(strict, a[:L, :L], 0.0)
        a_ak = jnp.where(strict, a[:L, L:], 0.0)
        a_rb = jnp.where(incl, a[L:, :L], 0.0)
        a_rk = jnp.where(incl, a[L:, L:], 0.0)
        x = -a_ab
        t = eye + x
        for _ in range(5):
            x = _dot_x3(x, x)
            t = t + _dot_x3(t, x)
        vh = vf[:, sl]
        akv = _dot(a_ak.astype(BF16), vh)
        rkv = _dot(a_rk.astype(BF16), vh)
        pq = _dot_x3(t, jnp.concatenate([kk_t[:, sl], akv], axis=1))
        rq_y0 = jnp.concatenate([r_t[:, sl], rkv], axis=1) - _dot(a_rb.astype(BF16), pq.astype(BF16))
        rq_o[0, :, sl] = rq_y0[:, :R_DH]
        y0_o[0, :, sl] = rq_y0[:, R_DH:]
        pqb = _dot_tn(pq.astype(BF16), b_w[:, sl].astype(BF16))
        m_o[0, 0, :, sl] = eye * w_last[:, sl] - pqb[:R_DH]
        n_o[0, 0, :, sl] = _dot_tn(vh, k_w[:, sl].astype(BF16)) - pqb[R_DH:]


def _rwkv_chunks(r, lw, k, v, kk, b):
    bsz, s, w = r.shape
    L = R_CHUNK
    nc = s // L
    tok = lambda bi, c: (bi, c, 0)
    st = lambda bi, c: (bi, c, 0, 0)
    return pl.pallas_call(
        _rchunk_kernel,
        grid=(bsz, nc),
        in_specs=[pl.BlockSpec((1, L, w), tok)] * 6,
        out_specs=[pl.BlockSpec((1, 1, R_DH, w), st), pl.BlockSpec((1, 1, R_DH, w), st),
                   pl.BlockSpec((1, L, w), tok), pl.BlockSpec((1, L, w), tok)],
        out_shape=[jax.ShapeDtypeStruct((bsz, nc, R_DH, w), F32), jax.ShapeDtypeStruct((bsz, nc, R_DH, w), F32),
                   jax.ShapeDtypeStruct((bsz, s, w), F32), jax.ShapeDtypeStruct((bsz, s, w), F32)],
        compiler_params=_cp("parallel", "parallel"),
        name="rwkv_chunks",
    )(r, lw, k, v, kk, b)


def _rscan_kernel(m_ref, n_ref, rq_ref, y0_ref, bv_ref, g_ref, lng_ref, lnb_ref, o_ref, s_sc):
    @pl.when(pl.program_id(1) == 0)
    def _():
        s_sc[...] = jnp.zeros(s_sc.shape, F32)

    for h in range(R_HEADS):
        sl = slice(h * R_DH, (h + 1) * R_DH)
        st = s_sc[:, sl]
        sh, slo = _split(st)
        rq = rq_ref[0, :, sl]
        rh, rlo = _split(rq)
        y = y0_ref[0, :, sl] + _dot_nt(rh, sh) + (_dot_nt(rlo, sh) + _dot_nt(rh, slo))
        s_sc[:, sl] = _dot_x3(st, m_ref[0, 0, :, sl]) + n_ref[0, 0, :, sl]
        mu = jnp.mean(y, axis=1, keepdims=True)
        yc = y - mu
        var = jnp.mean(yc * yc, axis=1, keepdims=True)
        yn = yc * lax.rsqrt(var + R_GN_EPS) * lng_ref[:, sl] + lnb_ref[:, sl]
        o_ref[0, :, sl] = ((yn + bv_ref[0, :, sl].astype(F32)) * g_ref[0, :, sl].astype(F32)).astype(o_ref.dtype)


def _rwkv_scan(m, n, rq, y0, bv, g, lng, lnb):
    bsz, s, w = rq.shape
    L = R_CHUNK
    tok = lambda bi, c: (bi, c, 0)
    st = lambda bi, c: (bi, c, 0, 0)
    full = lambda bi, c: (0, 0)
    return pl.pallas_call(
        _rscan_kernel,
        grid=(bsz, s // L),
        in_specs=[pl.BlockSpec((1, 1, R_DH, w), st), pl.BlockSpec((1, 1, R_DH, w), st),
                  pl.BlockSpec((1, L, w), tok), pl.BlockSpec((1, L, w), tok),
                  pl.BlockSpec((1, L, w), tok), pl.BlockSpec((1, L, w), tok),
                  pl.BlockSpec((1, w), full), pl.BlockSpec((1, w), full)],
        out_specs=pl.BlockSpec((1, L, w), tok),
        out_shape=jax.ShapeDtypeStruct((bsz, s, w), BF16),
        scratch_shapes=[pltpu.VMEM((R_DH, w), F32)],
        compiler_params=_cp("parallel", "arbitrary"),
        name="rwkv_scan",
    )(m, n, rq, y0, bv, g, lng, lnb)


def _merge_kernel(ha_ref, hb_ref, y_ref, gate_ref, x_ref, mup_ref, fup_ref, rup_ref, wo_ref, g_ref, b_ref,
                  o32_ref, o16_ref):
    d = D_MODEL
    merged = (_sigmoid(gate_ref[:, 0:d].astype(F32)) * _dot(ha_ref[...], mup_ref[...])
              + _sigmoid(gate_ref[:, d:2 * d].astype(F32)) * _dot(hb_ref[...], fup_ref[...])
              + _sigmoid(gate_ref[:, 2 * d:3 * d].astype(F32)) * _dot(y_ref[...], rup_ref[...]))
    mix = _dot(merged.astype(BF16), wo_ref[...])
    out = _ln(ALPHA * x_ref[...] + mix, g_ref[...], b_ref[...])
    o32_ref[...] = out
    o16_ref[...] = out.astype(BF16)


def _merge(ha, hb, y, p16, x32, mup, fup, rup, wo, g, b, tm):
    t, d = x32.shape
    tm = min(tm, t)
    full = lambda i: (0, 0)
    tok = lambda i: (i, 0)
    return pl.pallas_call(
        _merge_kernel,
        grid=(t // tm,),
        in_specs=[pl.BlockSpec((tm, 512), tok), pl.BlockSpec((tm, 512), tok), pl.BlockSpec((tm, 512), tok),
                  pl.BlockSpec((tm, 3 * d), lambda i: (i, C_GATE // (3 * d))),
                  pl.BlockSpec((tm, d), tok),
                  pl.BlockSpec((512, d), full), pl.BlockSpec((512, d), full), pl.BlockSpec((512, d), full),
                  pl.BlockSpec((d, d), full), pl.BlockSpec((1, d), full), pl.BlockSpec((1, d), full)],
        out_specs=[pl.BlockSpec((tm, d), tok), pl.BlockSpec((tm, d), tok)],
        out_shape=[jax.ShapeDtypeStruct((t, d), F32), jax.ShapeDtypeStruct((t, d), BF16)],
        compiler_params=_cp("parallel"),
        name="merge_ln1",
    )(ha, hb, y, p16, x32, mup, fup, rup, wo, g, b)


def _xattn_kernel(x16_ref, x32_ref, k_ref, v_ref, wq_ref, wo_ref, g_ref, b_ref, o32_ref, o16_ref):
    q = _dot(x16_ref[0], wq_ref[...]).astype(BF16)
    outs = []
    for h in range(X_HEADS):
        sl = slice(h * X_DH, (h + 1) * X_DH)
        lg = _dot_nt(q[:, sl], k_ref[0, :, sl]) * (X_DH ** -0.5)
        mx = jnp.max(lg, axis=1, keepdims=True)
        e = jnp.exp(lg - mx)
        p = e / jnp.sum(e, axis=1, keepdims=True)
        outs.append(_dot(p.astype(BF16), v_ref[0, :, sl]))
    o = jnp.concatenate(outs, axis=1).astype(BF16)
    out = _ln(ALPHA * x32_ref[0] + _dot(o, wo_ref[...]), g_ref[...], b_ref[...])
    o32_ref[0] = out
    o16_ref[0] = out.astype(BF16)


def _xattn(x16, x32, kk, vv, wq, wo, g, b, tm):
    bsz, s, d = x32.shape
    tm = min(tm, s)
    m = kk.shape[1]
    full = lambda bi, i: (0, 0)
    tok = lambda bi, i: (bi, i, 0)
    mem = lambda bi, i: (bi, 0, 0)
    return pl.pallas_call(
        _xattn_kernel,
        grid=(bsz, s // tm),
        in_specs=[pl.BlockSpec((1, tm, d), tok), pl.BlockSpec((1, tm, d), tok),
                  pl.BlockSpec((1, m, d), mem), pl.BlockSpec((1, m, d), mem),
                  pl.BlockSpec((d, d), full), pl.BlockSpec((d, d), full),
                  pl.BlockSpec((1, d), full), pl.BlockSpec((1, d), full)],
        out_specs=[pl.BlockSpec((1, tm, d), tok), pl.BlockSpec((1, tm, d), tok)],
        out_shape=[jax.ShapeDtypeStruct((bsz, s, d), F32), jax.ShapeDtypeStruct((bsz, s, d), BF16)],
        compiler_params=_cp("parallel", "parallel"),
        name="xattn_ln2",
    )(x16, x32, kk, vv, wq, wo, g, b)


def _ffn_kernel(x16_ref, x32_ref, wg_ref, wu_ref, wd_ref, g_ref, b_ref, o32_ref, o16_ref, acc_sc):
    j = pl.program_id(1)

    @pl.when(j == 0)
    def _():
        acc_sc[...] = jnp.zeros(acc_sc.shape, F32)

    x = x16_ref[...]
    act = _silu(_dot(x, wg_ref[...])) * _dot(x, wu_ref[...])
    acc_sc[...] += _dot(act.astype(BF16), wd_ref[...])

    @pl.when(j == pl.num_programs(1) - 1)
    def _():
        out = _ln(ALPHA * x32_ref[...] + acc_sc[...], g_ref[...], b_ref[...])
        o32_ref[...] = out
        o16_ref[...] = out.astype(BF16)


def _ffn(x16, x32, wgu, wd, g, b, tm, bf):
    t, d = x32.shape
    tm = min(tm, t)
    dff = wd.shape[0]
    nf = dff // bf
    tok = lambda i, j: (i, 0)
    full = lambda i, j: (0, 0)
    return pl.pallas_call(
        _ffn_kernel,
        grid=(t // tm, nf),
        in_specs=[pl.BlockSpec((tm, d), tok), pl.BlockSpec((tm, d), tok),
                  pl.BlockSpec((d, bf), lambda i, j: (0, j)),
                  pl.BlockSpec((d, bf), lambda i, j: (0, j + nf)),
                  pl.BlockSpec((bf, d), lambda i, j: (j, 0)),
                  pl.BlockSpec((1, d), full), pl.BlockSpec((1, d), full)],
        out_specs=[pl.BlockSpec((tm, d), tok), pl.BlockSpec((tm, d), tok)],
        out_shape=[jax.ShapeDtypeStruct((t, d), F32), jax.ShapeDtypeStruct((t, d), BF16)],
        scratch_shapes=[pltpu.VMEM((tm, d), F32)],
        compiler_params=_cp("parallel", "arbitrary"),
        name="ffn_ln3",
    )(x16, x32, wgu, wgu, wd, g, b)


def _moe_kernel(x16_ref, x32_ref, rw_ref, rb_ref, wg_ref, wu_ref, wd_ref, g_ref, b_ref, o_ref, acc_sc, comb_sc):
    e = pl.program_id(1)
    j = pl.program_id(2)
    first = jnp.logical_and(e == 0, j == 0)
    last = jnp.logical_and(e == pl.num_programs(1) - 1, j == pl.num_programs(2) - 1)

    @pl.when(first)
    def _():
        acc_sc[...] = jnp.zeros(acc_sc.shape, F32)
        lane = lax.broadcasted_iota(jnp.int32, comb_sc.shape, 1).astype(F32)
        logits = _dot_x3(x32_ref[...], rw_ref[...]) + rb_ref[...]
        logits = jnp.where(lane < N_EXPERTS, logits, -jnp.inf)
        m1 = jnp.max(logits, axis=1, keepdims=True)
        i1 = jnp.min(jnp.where(logits == m1, lane, 128.0), axis=1, keepdims=True)
        rest = jnp.where(lane == i1, -jnp.inf, logits)
        m2 = jnp.max(rest, axis=1, keepdims=True)
        i2 = jnp.min(jnp.where(rest == m2, lane, 128.0), axis=1, keepdims=True)
        e2 = jnp.exp(m2 - m1)
        w1 = 1.0 / (1.0 + e2)
        w2 = e2 / (1.0 + e2)
        comb_sc[...] = jnp.where(lane == i1, w1, 0.0) + jnp.where(lane == i2, w2, 0.0)

    lane = lax.broadcasted_iota(jnp.int32, comb_sc.shape, 1)
    ce = jnp.sum(jnp.where(lane == e, comb_sc[...], 0.0), axis=1, keepdims=True)
    x = x16_ref[...]
    act = _silu(_dot(x, wg_ref[0])) * _dot(x, wu_ref[0])
    acc_sc[...] += ce * _dot(act.astype(BF16), wd_ref[0])

    @pl.when(last)
    def _():
        o_ref[...] = _ln(ALPHA * x32_ref[...] + acc_sc[...], g_ref[...], b_ref[...])


def _moe(x16, x32, rw, rb, wgu, wd, g, b, tm, bf):
    t, d = x32.shape
    tm = min(tm, t)
    ne, dff, _ = wd.shape
    nf = dff // bf
    tok = lambda i, e, j: (i, 0)
    full = lambda i, e, j: (0, 0)
    return pl.pallas_call(
        _moe_kernel,
        grid=(t // tm, ne, nf),
        in_specs=[pl.BlockSpec((tm, d), tok), pl.BlockSpec((tm, d), tok),
                  pl.BlockSpec((d, 128), full), pl.BlockSpec((1, 128), full),
                  pl.BlockSpec((1, d, bf), lambda i, e, j: (e, 0, j)),
                  pl.BlockSpec((1, d, bf), lambda i, e, j: (e, 0, j + nf)),
                  pl.BlockSpec((1, bf, d), lambda i, e, j: (e, j, 0)),
                  pl.BlockSpec((1, d), full), pl.BlockSpec((1, d), full)],
        out_specs=pl.BlockSpec((tm, d), tok),
        out_shape=jax.ShapeDtypeStruct((t, d), F32),
        scratch_shapes=[pltpu.VMEM((tm, d), F32), pltpu.VMEM((tm, 128), F32)],
        compiler_params=_cp("parallel", "arbitrary", "arbitrary"),
        name="moe_ln3",
    )(x16, x32, rw, rb, wgu, wgu, wd, g, b)


def _src_layout(with_vres):
    cols = [('m_qk', 512), ('m_v', 512), ('m_o', 512), ('m_i', 4), ('m_f', 4),
            ('f_q', 512), ('f_k', 512), ('f_v', 512), ('f_f', 8), ('gate', 3072),
            ('r_r', 512), ('r_k', 512), ('r_v', 512), ('r_w', 64), ('r_a', 64), ('r_g', 128)]
    if with_vres:
        cols.append(('r_vres', 32))
    layout, start = {}, 0
    for name, width in cols:
        layout[name] = (start, start + width)
        start += width
    return layout


def _relayout_cols(w, with_vres):
    lay = _src_layout(with_vres)
    lead = w.shape[:-1]
    z = lambda n: jnp.zeros(lead + (n,), w.dtype)
    c = lambda name: w[..., lay[name][0]:lay[name][1]]
    parts = [c('m_qk'), c('m_v'), c('m_o'), c('f_q'), c('f_k'), c('f_v'), c('gate'),
             c('r_r'), c('r_k'), c('r_v'), c('r_w'), c('r_a'), c('r_g'),
             c('m_i'), c('m_f'), c('f_f'), z(16), c('r_vres') if with_vres else z(32), z(64), z(128)]
    return jnp.concatenate(parts, axis=-1)


def _row(v):
    return v.reshape(1, -1).astype(F32)


def _token_mixer(x16, x32, bsz, s, prm, v_first):
    (w_in, b_in, m_conv, m_norm, m_up, f_up, r_mu, r_wbias, r_wB, r_abias, r_aB, r_vbias, r_vB,
     r_gB, r_kk, r_ka, r_rk, r_ln_g, r_ln_b, r_up, w_out, ln1_g, ln1_b) = prm
    with_vres = r_vbias is not None
    t = bsz * s
    w_all = _relayout_cols(w_in, with_vres)
    b_all = _relayout_cols(b_in, with_vres).reshape(1, NP)
    p16 = _matmul_bias(x16, w_all.astype(BF16), b_all, BF16, 1024, 1024, "in_proj")
    gates = _matmul_bias(x16, w_all[:, C_SMALL:C_SMALL + 128].astype(BF16), b_all[:, C_SMALL:C_SMALL + 128],
                         F32, 2048, 128, "in_proj_gates")
    p16 = p16.reshape(bsz, s, NP)
    gates = gates.reshape(bsz, s, 128)
    gates_t = jnp.transpose(gates, (0, 2, 1))

    h_a = _mlstm(p16, gates, gates_t, m_conv.astype(F32), _row(m_norm), 256)
    cum = _fox_cumsum(gates_t)
    h_b = _fox_attention(p16, cum, jnp.transpose(cum, (0, 2, 1)), 512)
    lay = _src_layout(with_vres)
    r0 = lay['r_r'][0]
    mu_parts = [r_mu[:1792], jnp.zeros((32,), F32),
                r_mu[lay['r_vres'][0] - r0:] if with_vres else jnp.zeros((32,), F32), jnp.zeros((192,), F32)]
    mu = jnp.concatenate(mu_parts).reshape(1, 2048)
    z = jnp.zeros((64, 512), F32)
    w_wa = jnp.concatenate([jnp.concatenate([r_wB, z], axis=1), jnp.concatenate([z, r_aB], axis=1)], axis=0).astype(BF16)
    w_v = None
    if with_vres:
        w_v = jnp.concatenate([jnp.zeros((32, 512), F32), r_vB, jnp.zeros((64, 512), F32)], axis=0).astype(BF16)
    hid = jnp.arange(512) // R_DH
    headsum = (hid[:, None] == hid[None, :]).astype(BF16)
    outs = _rwkv_prep(p16, v_first, mu, w_wa, w_v, r_gB.astype(BF16), headsum, _row(r_wbias), _row(r_abias),
                      _row(r_vbias) if with_vres else None, _row(r_kk), _row(r_ka), _row(r_rk), 256)
    r, lw, k2, v, kk, b, g, bv = outs[:8]
    v_own = v_first if with_vres else outs[8]
    m_c, n_c, rq, y0 = _rwkv_chunks(r, lw, k2, v, kk, b)
    y = _rwkv_scan(m_c, n_c, rq, y0, bv, g, _row(r_ln_g), _row(r_ln_b))

    x1_32, x1_16 = _merge(h_a.reshape(t, 512), h_b.reshape(t, 512), y.reshape(t, 512), p16.reshape(t, NP), x32,
                          m_up.astype(BF16), f_up.astype(BF16), r_up.astype(BF16), w_out.astype(BF16),
                          _row(ln1_g), _row(ln1_b), 512)
    return x1_32, x1_16, v_own


def kernel(x, mem, mem_ln_g, mem_ln_b, w_in_0, b_in_0, m_conv_0, m_norm_0, m_up_0, f_up_0, r_mu_0, r_wbias_0, r_wB_0, r_abias_0, r_aB_0, r_gB_0, r_kk_0, r_ka_0, r_rk_0, r_ln_g_0, r_ln_b_0, r_up_0, w_out_0, ln1_g_0, ln1_b_0, x_wq_0, x_wkv_0, x_wo_0, ln2_g_0, ln2_b_0, ff_wgu_0, ff_wd_0, ln3_g_0, ln3_b_0, w_in_1, b_in_1, m_conv_1, m_norm_1, m_up_1, f_up_1, r_mu_1, r_wbias_1, r_wB_1, r_abias_1, r_aB_1, r_vbias_1, r_vB_1, r_gB_1, r_kk_1, r_ka_1, r_rk_1, r_ln_g_1, r_ln_b_1, r_up_1, w_out_1, ln1_g_1, ln1_b_1, x_wq_1, x_wkv_1, x_wo_1, ln2_g_1, ln2_b_1, ex_router_1, ex_router_b_1, ex_wgu_1, ex_wd_1, ln3_g_1, ln3_b_1):
    bsz, s, d = x.shape
    t = bsz * s
    x32 = x.reshape(t, d)
    x16 = x32.astype(BF16)

    mixers = (
        (w_in_0, b_in_0, m_conv_0, m_norm_0, m_up_0, f_up_0, r_mu_0, r_wbias_0, r_wB_0, r_abias_0, r_aB_0, None, None,
         r_gB_0, r_kk_0, r_ka_0, r_rk_0, r_ln_g_0, r_ln_b_0, r_up_0, w_out_0, ln1_g_0, ln1_b_0),
        (w_in_1, b_in_1, m_conv_1, m_norm_1, m_up_1, f_up_1, r_mu_1, r_wbias_1, r_wB_1, r_abias_1, r_aB_1, r_vbias_1, r_vB_1,
         r_gB_1, r_kk_1, r_ka_1, r_rk_1, r_ln_g_1, r_ln_b_1, r_up_1, w_out_1, ln1_g_1, ln1_b_1),
    )
    xattn = ((x_wq_0, x_wo_0, ln2_g_0, ln2_b_0), (x_wq_1, x_wo_1, ln2_g_1, ln2_b_1))
    kv0, kv1 = _mem_kv(mem, _row(mem_ln_g), _row(mem_ln_b), x_wkv_0.astype(BF16), x_wkv_1.astype(BF16))
    kvs = (kv0, kv1)

    v_first = None
    for l in range(DEPTH):
        x32, x16, v_own = _token_mixer(x16, x32, bsz, s, mixers[l], v_first)
        if l == 0:
            v_first = v_own
        wq, wo, g2, b2 = xattn[l]
        kv = kvs[l]
        x32, x16 = _xattn(x16.reshape(bsz, s, d), x32.reshape(bsz, s, d), kv[:, :, :d], kv[:, :, d:],
                          wq.astype(BF16), wo.astype(BF16), _row(g2), _row(b2), 512)
        x32, x16 = x32.reshape(t, d), x16.reshape(t, d)
        if l % 2 == 0:
            x32, x16 = _ffn(x16, x32, ff_wgu_0.astype(BF16), ff_wd_0.astype(BF16), _row(ln3_g_0), _row(ln3_b_0), 512, 1408)
        else:
            rw = jnp.concatenate([ex_router_1, jnp.zeros((d, 128 - N_EXPERTS), F32)], axis=1)
            rb = jnp.concatenate([ex_router_b_1, jnp.zeros((128 - N_EXPERTS,), F32)]).reshape(1, 128)
            x32 = _moe(x16, x32, rw, rb, ex_wgu_1.astype(BF16), ex_wd_1.astype(BF16), _row(ln3_g_1), _row(ln3_b_1), 1024, 896)
    return x32.reshape(bsz, s, d)
```

```python
import functools

import jax
import jax.numpy as jnp
from jax import lax
from jax.experimental import pallas as pl
from jax.experimental.pallas import tpu as pltpu

F32 = jnp.float32
BF16 = jnp.bfloat16

D_MODEL = 1024
DEPTH = 2
M_HEADS, M_DV, M_DQK, M_CONV, M_CHUNK = 4, 128, 64, 4, 64
F_HEADS, F_DH, F_PAD = 8, 64, 128
R_HEADS, R_DH, R_CHUNK = 8, 64, 64
R_GN_EPS = 64e-5
X_HEADS, X_DH = 4, 256
N_EXPERTS = 8
ALPHA = (2.0 * DEPTH) ** 0.25
LN_EPS = 1e-5
HEAD_NORM_EPS = 1e-6
NEG = -1e30
LOG2E = 1.4426950408889634

C_GATE = 0
C_RKV = 3072
C_RMISC = 4608
C_SMALL = 4864
C_FQ = 5120
C_FK = 6144
C_MQK = 7168
C_MV = 7680
C_MO = 8192
C_FV = 8704
NP = 9216
VMEM_LIMIT = 56 * 1024 * 1024


def _cp(*sem):
    return pltpu.CompilerParams(dimension_semantics=sem, vmem_limit_bytes=VMEM_LIMIT)


def _dot(a, b):
    return jnp.dot(a, b, preferred_element_type=F32)


def _dot_nt(a, b):
    return lax.dot_general(a, b, (((1,), (1,)), ((), ())), preferred_element_type=F32)


def _dot_tn(a, b):
    return lax.dot_general(a, b, (((0,), (0,)), ((), ())), preferred_element_type=F32)


def _split(x):
    hi = x.astype(BF16)
    lo = (x - hi.astype(F32)).astype(BF16)
    return hi, lo


def _split3(x):
    h1 = x.astype(BF16)
    r1 = x - h1.astype(F32)
    h2 = r1.astype(BF16)
    h3 = (r1 - h2.astype(F32)).astype(BF16)
    return h1, h2, h3


def _dot_x2(a, b16):
    hi, lo = _split(a)
    return _dot(hi, b16) + _dot(lo, b16)


def _dot_x3(a, b):
    ah, al = _split(a)
    bh, bl = _split(b)
    return _dot(ah, bh) + (_dot(al, bh) + _dot(ah, bl))


def _dot16(a, b):
    return _dot(a.astype(BF16), b.astype(BF16))


def _sigmoid(x):
    return 1.0 / (1.0 + jnp.exp(-x))


def _log_sigmoid(x):
    return jnp.minimum(x, 0.0) - jnp.log(1.0 + jnp.exp(-jnp.abs(x)))


def _softplus(x):
    return jnp.maximum(x, 0.0) + jnp.log(1.0 + jnp.exp(-jnp.abs(x)))


def _silu(x):
    return x * _sigmoid(x)


def _ln(z, g, b, eps=LN_EPS):
    mu = jnp.mean(z, axis=-1, keepdims=True)
    zc = z - mu
    var = jnp.mean(zc * zc, axis=-1, keepdims=True)
    return zc * lax.rsqrt(var + eps) * g + b


def _mm_kernel(a_ref, w_ref, b_ref, o_ref):
    o_ref[...] = (_dot(a_ref[...], w_ref[...]) + b_ref[...]).astype(o_ref.dtype)


def _matmul_bias(a, w, b, out_dtype, tm, tn, name):
    m, k = a.shape
    n = w.shape[1]
    tm, tn = min(tm, m), min(tn, n)
    return pl.pallas_call(
        _mm_kernel,
        grid=(m // tm, n // tn),
        in_specs=[pl.BlockSpec((tm, k), lambda i, j: (i, 0)),
                  pl.BlockSpec((k, tn), lambda i, j: (0, j)),
                  pl.BlockSpec((1, tn), lambda i, j: (0, j))],
        out_specs=pl.BlockSpec((tm, tn), lambda i, j: (i, j)),
        out_shape=jax.ShapeDtypeStruct((m, n), out_dtype),
        compiler_params=_cp("parallel", "parallel"),
        name=name,
    )(a, w, b)


def _memkv_kernel(mem_ref, g_ref, b_ref, w0_ref, w1_ref, o0_ref, o1_ref):
    mn = _ln(mem_ref[0], g_ref[...], b_ref[...]).astype(BF16)
    o0_ref[0] = _dot(mn, w0_ref[...]).astype(BF16)
    o1_ref[0] = _dot(mn, w1_ref[...]).astype(BF16)


def _mem_kv(mem, g, b, wkv0, wkv1):
    bsz, m, d = mem.shape
    n = wkv0.shape[1]
    full = lambda i: (0, 0)
    return pl.pallas_call(
        _memkv_kernel,
        grid=(bsz,),
        in_specs=[pl.BlockSpec((1, m, d), lambda i: (i, 0, 0)),
                  pl.BlockSpec((1, d), full), pl.BlockSpec((1, d), full),
                  pl.BlockSpec((d, n), full), pl.BlockSpec((d, n), full)],
        out_specs=[pl.BlockSpec((1, m, n), lambda i: (i, 0, 0))] * 2,
        out_shape=[jax.ShapeDtypeStruct((bsz, m, n), BF16)] * 2,
        compiler_params=_cp("parallel"),
        name="mem_kv",
    )(mem, g, b, wkv0, wkv1)


def _foxprep_kernel(g_ref, q_ref, k_ref, v_ref, qo_ref, ko_ref, vo_ref, carry_sc, *, tt):
    blk = 128

    @pl.when(pl.program_id(1) == 0)
    def _():
        carry_sc[...] = jnp.zeros(carry_sc.shape, F32)

    r = lax.broadcasted_iota(jnp.int32, (blk, blk), 0)
    c = lax.broadcasted_iota(jnp.int32, (blk, blk), 1)
    upper = (r <= c).astype(BF16)
    carry = carry_sc[:, 0:1]
    parts = []
    for i in range(tt // blk):
        ls = _log_sigmoid(g_ref[0, :, i * blk:(i + 1) * blk])
        h1, h2, h3 = _split3(ls)
        cs = _dot(h1, upper) + (_dot(h2, upper) + _dot(h3, upper)) + carry
        carry = cs[:, blk - 1:blk]
        parts.append(cs)
    carry_sc[...] = jnp.broadcast_to(carry, carry_sc.shape)
    cum = jnp.concatenate(parts, axis=1) * LOG2E
    c1, c2, c3 = [p.astype(F32) for p in _split3(cum)]
    w = F_HEADS * F_PAD
    row = lax.broadcasted_iota(jnp.int32, (F_HEADS, w), 0)
    lane = lax.broadcasted_iota(jnp.int32, (F_HEADS, w), 1)
    sel = lambda j: (lane == row * F_PAD + F_DH + j).astype(F32)
    lane1 = lax.broadcasted_iota(jnp.int32, (1, w), 1) % F_PAD
    ones_q = jnp.logical_and(lane1 >= F_DH + 3, lane1 < F_DH + 6).astype(F32)
    ones_k = jnp.logical_and(lane1 >= F_DH, lane1 < F_DH + 3).astype(F32)
    aug_q = _dot_tn(c1, sel(0)) + _dot_tn(c2, sel(1)) + _dot_tn(c3, sel(2)) + ones_q
    aug_k = ones_k - (_dot_tn(c1, sel(3)) + _dot_tn(c2, sel(4)) + _dot_tn(c3, sel(5)))
    qo_ref[0] = (q_ref[0].astype(F32) + aug_q).astype(BF16)
    ko_ref[0] = (k_ref[0].astype(F32) + aug_k).astype(BF16)
    wv = F_HEADS * F_DH
    src = lax.broadcasted_iota(jnp.int32, (wv, w), 0)
    dst = lax.broadcasted_iota(jnp.int32, (wv, w), 1)
    place = (dst == (src // F_DH) * F_PAD + src % F_DH).astype(BF16)
    ones_v = (lane1 >= F_DH).astype(F32)
    vo_ref[0] = jnp.transpose(_dot(v_ref[0], place) + ones_v).astype(BF16)


def _fox_prep(p16, gates_t, tt):
    bsz, s, _ = p16.shape
    tt = min(tt, s)
    w = F_HEADS * F_PAD
    wv = F_HEADS * F_DH
    tok = lambda b, i: (b, i, 0)
    return pl.pallas_call(
        functools.partial(_foxprep_kernel, tt=tt),
        grid=(bsz, s // tt),
        in_specs=[pl.BlockSpec((1, 8, tt), lambda b, i: (b, 1, i)),
                  pl.BlockSpec((1, tt, w), lambda b, i: (b, i, C_FQ // w)),
                  pl.BlockSpec((1, tt, w), lambda b, i: (b, i, C_FK // w)),
                  pl.BlockSpec((1, tt, wv), lambda b, i: (b, i, C_FV // wv))],
        out_specs=[pl.BlockSpec((1, tt, w), tok), pl.BlockSpec((1, tt, w), tok),
                   pl.BlockSpec((1, w, tt), lambda b, i: (b, 0, i))],
        out_shape=[jax.ShapeDtypeStruct((bsz, s, w), BF16), jax.ShapeDtypeStruct((bsz, s, w), BF16),
                   jax.ShapeDtypeStruct((bsz, w, s), BF16)],
        scratch_shapes=[pltpu.VMEM((8, 128), F32)],
        compiler_params=_cp("parallel", "arbitrary"),
        name="fox_prep",
    )(gates_t, p16, p16, p16)


def _fox_kernel(q_ref, k_ref, v_ref, o_ref, m_sc, acc_sc, *, t):
    qi = pl.program_id(1)
    ki = pl.program_id(2)

    @pl.when(ki == 0)
    def _():
        m_sc[...] = jnp.full(m_sc.shape, NEG, F32)
        acc_sc[...] = jnp.zeros(acc_sc.shape, F32)

    def block(diag):
        if diag:
            key = lax.broadcasted_iota(jnp.int32, (t, t), 0)
            qry = lax.broadcasted_iota(jnp.int32, (t, t), 1)
            mask = key <= qry
        pss = [slice(h * F_PAD, (h + 1) * F_PAD) for h in range(F_HEADS)]
        scores = lambda h: _dot_nt(k_ref[0, :, pss[h]], q_ref[0, :, pss[h]])
        s_next = scores(0)
        for h in range(F_HEADS):
            s = s_next
            if h + 1 < F_HEADS:
                s_next = scores(h + 1)
            if diag:
                s = jnp.where(mask, s, NEG)
            m_old = m_sc[h]
            m_new = jnp.maximum(m_old, jnp.max(s, axis=0, keepdims=True))
            p = jnp.exp2(s - m_new).astype(BF16)
            acc_sc[h] = jnp.exp2(m_old - m_new) * acc_sc[h] + _dot(v_ref[0, pss[h], :], p)
            m_sc[h] = m_new

    @pl.when(ki < qi)
    def _():
        block(False)

    @pl.when(ki == qi)
    def _():
        block(True)
        for h in range(F_HEADS):
            acc = acc_sc[h]
            out_t = acc[:F_DH, :] / acc[F_DH:F_DH + 1, :]
            o_ref[0, :, h * F_DH:(h + 1) * F_DH] = jnp.transpose(out_t).astype(o_ref.dtype)


def _fox_attention(q_aug, k_aug, v_aug_t, t):
    bsz, s, w = q_aug.shape
    t = min(t, s)
    n = s // t
    wv = F_HEADS * F_DH
    return pl.pallas_call(
        functools.partial(_fox_kernel, t=t),
        grid=(bsz, n, n),
        in_specs=[pl.BlockSpec((1, t, w), lambda b, i, j: (b, i, 0)),
                  pl.BlockSpec((1, t, w), lambda b, i, j: (b, jnp.minimum(i, j), 0)),
                  pl.BlockSpec((1, w, t), lambda b, i, j: (b, 0, jnp.minimum(i, j)))],
        out_specs=pl.BlockSpec((1, t, wv), lambda b, i, j: (b, i, 0)),
        out_shape=jax.ShapeDtypeStruct((bsz, s, wv), BF16),
        scratch_shapes=[pltpu.VMEM((F_HEADS, 1, t), F32), pltpu.VMEM((F_HEADS, F_PAD, t), F32)],
        compiler_params=_cp("parallel", "parallel", "arbitrary"),
        name="fox_attention",
    )(q_aug, k_aug, v_aug_t)


def _mlstm_kernel(qk_ref, v_ref, o_ref, g_ref, gt_ref, cw_ref, nw_ref, out_ref,
                  prev_sc, q_sc, k_sc, c_sc, m_sc, *, lt):
    ti = pl.program_id(1)
    L = M_CHUNK
    H = range(M_HEADS)

    @pl.when(ti == 0)
    def _():
        prev_sc[...] = jnp.zeros(prev_sc.shape, F32)
        c_sc[...] = jnp.zeros(c_sc.shape, F32)
        m_sc[...] = jnp.zeros(m_sc.shape, F32)

    u = qk_ref[0].astype(F32)
    prev = prev_sc[...]
    rows = lax.broadcasted_iota(jnp.int32, u.shape, 0)
    acc = u * cw_ref[M_CONV - 1:M_CONV, :]
    for d in range(1, M_CONV):
        sh = jnp.where(rows < d, pltpu.roll(prev, d, 0), pltpu.roll(u, d, 0))
        acc = acc + sh * cw_ref[M_CONV - 1 - d:M_CONV - d, :]
    prev_sc[...] = u
    qk = _silu(acc)
    nq = M_HEADS * M_DQK
    q_sc[...] = qk[:, :nq].astype(BF16)
    k_sc[...] = qk[:, nq:] * (M_DQK ** -0.5)

    r = lax.broadcasted_iota(jnp.int32, (L, L), 0)
    c = lax.broadcasted_iota(jnp.int32, (L, L), 1)
    tri = c <= r
    lower = tri.astype(BF16)
    upper = (r <= c).astype(BF16)
    ones = jnp.ones((L, M_DV), BF16)

    for ch in range(lt // L):
        rs = slice(ch * L, (ch + 1) * L)
        gc = g_ref[0, rs, :]
        gr = gt_ref[0, :, rs]
        lc1, lc2 = _split(_log_sigmoid(gc))
        lr1, lr2 = _split(_log_sigmoid(gr))
        bcol = _dot(lower, lc1) + _dot(lower, lc2)
        brow = _dot(lr1, upper) + _dot(lr2, upper)
        hsl = [slice(h * M_DV, (h + 1) * M_DV) for h in H]
        qsl = [slice(h * M_DQK, (h + 1) * M_DQK) for h in H]
        bc = [bcol[:, 4 + h:5 + h] for h in H]
        m_prev = [m_sc[h:h + 1, 0:1] for h in H]
        qh = [q_sc[rs, qsl[h]] for h in H]
        kh = [k_sc[rs, qsl[h]] for h in H]
        vext = [jnp.concatenate([v_ref[0, rs, hsl[h]], ones], axis=1) for h in H]
        cext = [c_sc[h] for h in H]
        dm = [jnp.where(tri, bc[h] - brow[4 + h:5 + h, :] + gr[h:h + 1, :], -jnp.inf) for h in H]
        d_inter = [bc[h] + m_prev[h] for h in H]
        m_t = [jnp.maximum(d_inter[h], jnp.max(dm[h], axis=1, keepdims=True)) for h in H]
        qk_s = [_dot_nt(qh[h], kh[h].astype(BF16)) for h in H]
        qc = [_dot(qh[h], cext[h].astype(BF16)) for h in H]
        w_intra = [(qk_s[h] * jnp.exp(dm[h] - m_t[h])).astype(BF16) for h in H]
        ext = [jnp.exp(d_inter[h] - m_t[h]) * qc[h] + _dot(w_intra[h], vext[h]) for h in H]
        b_last = [bc[h][L - 1:L, :] for h in H]
        g = [b_last[h] - bc[h] + gc[:, h:h + 1] for h in H]
        m_new = [jnp.maximum(b_last[h] + m_prev[h], jnp.max(g[h], axis=0, keepdims=True)) for h in H]
        wk = [(jnp.exp(g[h] - m_new[h]) * kh[h]).astype(BF16) for h in H]
        upd = [_dot_tn(wk[h], vext[h]) for h in H]
        for h in H:
            c_sc[h] = jnp.exp(b_last[h] + m_prev[h] - m_new[h]) * cext[h] + upd[h]
            m_sc[h:h + 1, :] = jnp.broadcast_to(m_new[h], (1, m_sc.shape[1]))
        for h in H:
            hh = ext[h][:, :M_DV] / jnp.maximum(jnp.abs(ext[h][:, M_DV:M_DV + 1]), jnp.exp(-m_t[h]))
            mu = jnp.mean(hh, axis=1, keepdims=True)
            hc = hh - mu
            var = jnp.mean(hc * hc, axis=1, keepdims=True)
            hn = hc * lax.rsqrt(var + HEAD_NORM_EPS) * nw_ref[:, hsl[h]]
            out_ref[0, rs, hsl[h]] = (hn * _sigmoid(o_ref[0, rs, hsl[h]].astype(F32))).astype(out_ref.dtype)


def _mlstm(p16, gates, gates_t, conv_w, norm_w, lt):
    bsz, s, _ = p16.shape
    lt = min(lt, s)
    w = M_HEADS * M_DV
    full = lambda b, i: (0, 0)
    return pl.pallas_call(
        functools.partial(_mlstm_kernel, lt=lt),
        grid=(bsz, s // lt),
        in_specs=[pl.BlockSpec((1, lt, w), lambda b, i: (b, i, C_MQK // w)),
                  pl.BlockSpec((1, lt, w), lambda b, i: (b, i, C_MV // w)),
                  pl.BlockSpec((1, lt, w), lambda b, i: (b, i, C_MO // w)),
                  pl.BlockSpec((1, lt, 128), lambda b, i: (b, i, 0)),
                  pl.BlockSpec((1, 8, lt), lambda b, i: (b, 0, i)),
                  pl.BlockSpec((M_CONV, w), full),
                  pl.BlockSpec((1, w), full)],
        out_specs=pl.BlockSpec((1, lt, w), lambda b, i: (b, i, 0)),
        out_shape=jax.ShapeDtypeStruct((bsz, s, w), BF16),
        scratch_shapes=[pltpu.VMEM((lt, w), F32),
                        pltpu.VMEM((lt, M_HEADS * M_DQK), BF16), pltpu.VMEM((lt, M_HEADS * M_DQK), F32),
                        pltpu.VMEM((M_HEADS, M_DQK, 2 * M_DV), F32), pltpu.VMEM((8, 128), F32)],
        compiler_params=_cp("parallel", "arbitrary"),
        name="mlstm",
    )(p16, p16, p16, gates, gates_t, conv_w, norm_w)


def _rprep_kernel(*refs, tt, has_vres):
    if has_vres:
        (p_ref, pm_ref, vf_ref, mu_ref, mum_ref, wa_ref, wv_ref, wg_ref, hs_ref, wbias_ref, abias_ref, vbias_ref,
         kkw_ref, kaw_ref, rkw_ref,
         r_o, lw_o, k_o, v_o, kk_o, b_o, g_o, bv_o, carry_sc, carrym_sc) = refs
    else:
        (p_ref, pm_ref, mu_ref, mum_ref, wa_ref, wg_ref, hs_ref, wbias_ref, abias_ref,
         kkw_ref, kaw_ref, rkw_ref,
         r_o, lw_o, k_o, v_o, kk_o, b_o, g_o, bv_o, vown_o, carry_sc, carrym_sc) = refs
    ti = pl.program_id(1)

    @pl.when(ti == 0)
    def _():
        carry_sc[...] = jnp.zeros(carry_sc.shape, F32)
        carrym_sc[...] = jnp.zeros(carrym_sc.shape, F32)

    rows = lax.broadcasted_iota(jnp.int32, (tt, 512), 0)

    def shifted(ref, cref, mref, c0, width):
        x = ref[0, :, c0:c0 + width].astype(F32)
        prev = jnp.where(rows[:, :width] == 0, cref[0:1, c0:c0 + width], pltpu.roll(x, 1, 0))
        return x + mref[:, c0:c0 + width] * (prev - x)

    r = shifted(p_ref, carry_sc, mu_ref, 0, 512)
    k = shifted(p_ref, carry_sc, mu_ref, 512, 512)
    v = shifted(p_ref, carry_sc, mu_ref, 1024, 512)
    wa_in = shifted(pm_ref, carrym_sc, mum_ref, 0, 128)
    g_in = shifted(pm_ref, carrym_sc, mum_ref, 128, 128)
    lane = lax.broadcasted_iota(jnp.int32, (tt, 128), 1)
    wa_act = jnp.where(lane < 64, jnp.tanh(wa_in), wa_in).astype(BF16)
    wa = _dot(wa_act, wa_ref[...])
    w_log = -_softplus(-(wbias_ref[...] + wa[:, :512])) - 0.5
    lw_o[0] = -jnp.exp(w_log)
    a = _sigmoid(abias_ref[...] + wa[:, 512:])
    if has_vres:
        s_in = shifted(pm_ref, carrym_sc, mum_ref, 256, 128)
        mix = _sigmoid(vbias_ref[...] + _dot(s_in.astype(BF16), wv_ref[...]))
        v = v + (vf_ref[0].astype(F32) - v) * mix
    else:
        vown_o[0] = v.astype(vown_o.dtype)
    g_o[0] = _dot(_sigmoid(g_in).astype(BF16), wg_ref[...]).astype(g_o.dtype)
    hs = hs_ref[...]
    u = k * kkw_ref[...]
    nrm = jnp.sqrt(_dot_x2(u * u, hs))
    kk = u / jnp.maximum(nrm, 1e-12)
    k2 = k * (1.0 + (a - 1.0) * kaw_ref[...])
    bonus = _dot_x2(r * k2 * rkw_ref[...], hs)
    r_o[0] = r.astype(r_o.dtype)
    k_o[0] = k2.astype(k_o.dtype)
    v_o[0] = v.astype(v_o.dtype)
    kk_o[0] = kk.astype(kk_o.dtype)
    b_o[0] = (kk * a).astype(b_o.dtype)
    bv_o[0] = (bonus * v).astype(bv_o.dtype)
    carry_sc[0:1, :] = p_ref[0, tt - 1:tt, :].astype(F32)
    carrym_sc[0:1, :] = pm_ref[0, tt - 1:tt, :].astype(F32)


def _rwkv_prep(p16, v_first, mu, mum, w_wa, w_v, w_g, headsum, wbias, abias, vbias, kkw, kaw, rkw, tt):
    bsz, s, _ = p16.shape
    tt = min(tt, s)
    has_vres = v_first is not None
    full = lambda b, i: (0, 0)
    tok = lambda b, i: (b, i, 0)
    vec = pl.BlockSpec((1, 512), full)
    ins = [p16, p16]
    specs = [pl.BlockSpec((1, tt, 1536), lambda b, i: (b, i, C_RKV // 1536)),
             pl.BlockSpec((1, tt, 512), lambda b, i: (b, i, C_RMISC // 512))]
    if has_vres:
        ins.append(v_first)
        specs.append(pl.BlockSpec((1, tt, 512), tok))
    ins += [mu, mum, w_wa]
    specs += [pl.BlockSpec((1, 1536), full), pl.BlockSpec((1, 512), full), pl.BlockSpec((128, 1024), full)]
    if has_vres:
        ins.append(w_v)
        specs.append(pl.BlockSpec((128, 512), full))
    ins += [w_g, headsum, wbias, abias]
    specs += [pl.BlockSpec((128, 512), full), pl.BlockSpec((512, 512), full), vec, vec]
    if has_vres:
        ins.append(vbias)
        specs.append(vec)
    ins += [kkw, kaw, rkw]
    specs += [vec, vec, vec]
    n_out = 8 if has_vres else 9
    dts = [BF16, F32, BF16, BF16, BF16, BF16, BF16, BF16] + ([] if has_vres else [BF16])
    return pl.pallas_call(
        functools.partial(_rprep_kernel, tt=tt, has_vres=has_vres),
        grid=(bsz, s // tt),
        in_specs=specs,
        out_specs=[pl.BlockSpec((1, tt, 512), tok)] * n_out,
        out_shape=[jax.ShapeDtypeStruct((bsz, s, 512), dt) for dt in dts],
        scratch_shapes=[pltpu.VMEM((8, 1536), F32), pltpu.VMEM((8, 512), F32)],
        compiler_params=_cp("parallel", "arbitrary"),
        name="rwkv_prep",
    )(*ins)


def _rchunk_kernel(r_ref, lw_ref, k_ref, v_ref, kk_ref, b_ref, m_o, n_o, rq_o, y0_o, *, nck):
    L = R_CHUNK
    ri = lax.broadcasted_iota(jnp.int32, (L, L), 0)
    ci = lax.broadcasted_iota(jnp.int32, (L, L), 1)
    lower = (ci <= ri).astype(BF16)
    strict = ci < ri
    incl = ci <= ri
    eye = (ci == ri).astype(F32)

    kk_t, r_t, b_h, k_h, b_w, k_w, w_last, vf = [], [], [], [], [], [], [], []
    for c in range(nck):
        rs = slice(c * L, (c + 1) * L)
        lw = lw_ref[0, rs, :]
        l1, l2 = _split(lw)
        cl = _dot(lower, l1) + _dot(lower, l2)
        cl_last = cl[L - 1:L, :]
        w_inv = jnp.exp(-cl)
        w_rem = jnp.exp(cl_last - cl)
        kf = k_ref[0, rs, :].astype(F32)
        bf = b_ref[0, rs, :].astype(F32)
        kk_t.append((kk_ref[0, rs, :].astype(F32) * jnp.exp(cl - lw)).astype(BF16))
        r_t.append(r_ref[0, rs, :].astype(F32) * jnp.exp(cl))
        b_h.append((bf * w_inv).astype(BF16))
        k_h.append((kf * w_inv).astype(BF16))
        b_w.append((bf * w_rem).astype(BF16))
        k_w.append((kf * w_rem).astype(BF16))
        w_last.append(jnp.exp(cl_last))
        vf.append(v_ref[0, rs, :])

    J = [(c, slice(h * R_DH, (h + 1) * R_DH)) for c in range(nck) for h in range(R_HEADS)]
    N = range(len(J))
    a = [_dot_nt(jnp.concatenate([kk_t[c][:, sl], r_t[c][:, sl].astype(BF16)], axis=0),
                 jnp.concatenate([b_h[c][:, sl], k_h[c][:, sl]], axis=0)) for c, sl in J]
    a_ak = [jnp.where(strict, a[j][:L, L:], 0.0).astype(BF16) for j in N]
    a_rb = [jnp.where(incl, a[j][L:, :L], 0.0).astype(BF16) for j in N]
    a_rk = [jnp.where(incl, a[j][L:, L:], 0.0).astype(BF16) for j in N]
    x = [jnp.where(strict, -a[j][:L, :L], 0.0) for j in N]
    t = [eye + x[j] for j in N]
    for _ in range(5):
        x = [_dot16(x[j], x[j]) for j in N]
        t = [t[j] + _dot16(t[j], x[j]) for j in N]
    vh = [vf[c][:, sl] for c, sl in J]
    akv = [_dot(a_ak[j], vh[j]) for j in N]
    rkv = [_dot(a_rk[j], vh[j]) for j in N]
    pq16 = [_dot16(t[j], jnp.concatenate([kk_t[c][:, sl].astype(F32), akv[j]], axis=1)).astype(BF16)
            for j, (c, sl) in enumerate(J)]
    rq_y0 = [jnp.concatenate([r_t[c][:, sl], rkv[j]], axis=1) - _dot(a_rb[j], pq16[j]) for j, (c, sl) in enumerate(J)]
    pqb = [_dot_tn(pq16[j], b_w[c][:, sl]) for j, (c, sl) in enumerate(J)]
    vk = [_dot_tn(vh[j], k_w[c][:, sl]) for j, (c, sl) in enumerate(J)]
    for j, (c, sl) in enumerate(J):
        rs = slice(c * L, (c + 1) * L)
        rq_o[0, rs, sl] = rq_y0[j][:, :R_DH]
        y0_o[0, rs, sl] = rq_y0[j][:, R_DH:]
        h = j % R_HEADS
        m_o[0, c, h] = eye * w_last[c][:, sl] - pqb[j][:R_DH]
        n_o[0, c, h] = vk[j] - pqb[j][R_DH:]


def _rwkv_chunks(r, lw, k, v, kk, b, nck):
    bsz, s, w = r.shape
    L = R_CHUNK
    nc = s // L
    tok = lambda bi, c: (bi, c, 0)
    st = lambda bi, c: (bi, c, 0, 0, 0)
    sshape = (bsz, nc, R_HEADS, R_DH, R_DH)
    sblock = (1, nck, R_HEADS, R_DH, R_DH)
    return pl.pallas_call(
        functools.partial(_rchunk_kernel, nck=nck),
        grid=(bsz, nc // nck),
        in_specs=[pl.BlockSpec((1, nck * L, w), tok)] * 6,
        out_specs=[pl.BlockSpec(sblock, st), pl.BlockSpec(sblock, st),
                   pl.BlockSpec((1, nck * L, w), tok), pl.BlockSpec((1, nck * L, w), tok)],
        out_shape=[jax.ShapeDtypeStruct(sshape, F32), jax.ShapeDtypeStruct(sshape, F32),
                   jax.ShapeDtypeStruct((bsz, s, w), F32), jax.ShapeDtypeStruct((bsz, s, w), F32)],
        compiler_params=_cp("parallel", "parallel"),
        name="rwkv_chunks",
    )(r, lw, k, v, kk, b)


def _rscan_kernel(m_ref, n_ref, rq_ref, y0_ref, bv_ref, g_ref, lng_ref, lnb_ref, o_ref, s_sc, *, nck):
    L = R_CHUNK
    H = range(R_HEADS)
    sls = [slice(h * R_DH, (h + 1) * R_DH) for h in H]

    @pl.when(pl.program_id(1) == 0)
    def _():
        s_sc[...] = jnp.zeros(s_sc.shape, F32)

    for c in range(nck):
        rs = slice(c * L, (c + 1) * L)
        sp = [_split(s_sc[h]) for h in H]
        mp = [_split(m_ref[0, c, h]) for h in H]
        snew = [_dot(sp[h][0], mp[h][0]) + (_dot(sp[h][1], mp[h][0]) + _dot(sp[h][0], mp[h][1])) for h in H]
        for h in H:
            s_sc[h] = snew[h] + n_ref[0, c, h]
        rp = [_split(rq_ref[0, rs, sls[h]]) for h in H]
        y = [y0_ref[0, rs, sls[h]] + _dot_nt(rp[h][0], sp[h][0])
             + (_dot_nt(rp[h][1], sp[h][0]) + _dot_nt(rp[h][0], sp[h][1])) for h in H]
        for h in H:
            sl = sls[h]
            mu = jnp.mean(y[h], axis=1, keepdims=True)
            yc = y[h] - mu
            var = jnp.mean(yc * yc, axis=1, keepdims=True)
            yn = yc * lax.rsqrt(var + R_GN_EPS) * lng_ref[:, sl] + lnb_ref[:, sl]
            o_ref[0, rs, sl] = ((yn + bv_ref[0, rs, sl].astype(F32)) * g_ref[0, rs, sl].astype(F32)).astype(o_ref.dtype)


def _rwkv_scan(m, n, rq, y0, bv, g, lng, lnb, nck):
    bsz, s, w = rq.shape
    L = R_CHUNK
    tok = lambda bi, c: (bi, c, 0)
    st = lambda bi, c: (bi, c, 0, 0, 0)
    full = lambda bi, c: (0, 0)
    sblock = (1, nck, R_HEADS, R_DH, R_DH)
    return pl.pallas_call(
        functools.partial(_rscan_kernel, nck=nck),
        grid=(bsz, s // (L * nck)),
        in_specs=[pl.BlockSpec(sblock, st), pl.BlockSpec(sblock, st),
                  pl.BlockSpec((1, nck * L, w), tok), pl.BlockSpec((1, nck * L, w), tok),
                  pl.BlockSpec((1, nck * L, w), tok), pl.BlockSpec((1, nck * L, w), tok),
                  pl.BlockSpec((1, w), full), pl.BlockSpec((1, w), full)],
        out_specs=pl.BlockSpec((1, nck * L, w), tok),
        out_shape=jax.ShapeDtypeStruct((bsz, s, w), BF16),
        scratch_shapes=[pltpu.VMEM((R_HEADS, R_DH, R_DH), F32)],
        compiler_params=_cp("parallel", "arbitrary"),
        name="rwkv_scan",
    )(m, n, rq, y0, bv, g, lng, lnb)


def _merge_kernel(ha_ref, hb_ref, y_ref, gate_ref, x_ref, mup_ref, fup_ref, rup_ref, wo_ref, g_ref, b_ref,
                  o32_ref, o16_ref):
    d = D_MODEL
    merged = (_sigmoid(gate_ref[:, 0:d].astype(F32)) * _dot(ha_ref[...], mup_ref[...])
              + _sigmoid(gate_ref[:, d:2 * d].astype(F32)) * _dot(hb_ref[...], fup_ref[...])
              + _sigmoid(gate_ref[:, 2 * d:3 * d].astype(F32)) * _dot(y_ref[...], rup_ref[...]))
    mix = _dot(merged.astype(BF16), wo_ref[...])
    out = _ln(ALPHA * x_ref[...] + mix, g_ref[...], b_ref[...])
    o32_ref[...] = out
    o16_ref[...] = out.astype(BF16)


def _merge(ha, hb, y, p16, x32, mup, fup, rup, wo, g, b, tm):
    t, d = x32.shape
    tm = min(tm, t)
    full = lambda i: (0, 0)
    tok = lambda i: (i, 0)
    return pl.pallas_call(
        _merge_kernel,
        grid=(t // tm,),
        in_specs=[pl.BlockSpec((tm, 512), tok), pl.BlockSpec((tm, 512), tok), pl.BlockSpec((tm, 512), tok),
                  pl.BlockSpec((tm, 3 * d), lambda i: (i, C_GATE // (3 * d))),
                  pl.BlockSpec((tm, d), tok),
                  pl.BlockSpec((512, d), full), pl.BlockSpec((512, d), full), pl.BlockSpec((512, d), full),
                  pl.BlockSpec((d, d), full), pl.BlockSpec((1, d), full), pl.BlockSpec((1, d), full)],
        out_specs=[pl.BlockSpec((tm, d), tok), pl.BlockSpec((tm, d), tok)],
        out_shape=[jax.ShapeDtypeStruct((t, d), F32), jax.ShapeDtypeStruct((t, d), BF16)],
        compiler_params=_cp("parallel"),
        name="merge_ln1",
    )(ha, hb, y, p16, x32, mup, fup, rup, wo, g, b)


def _xattn_kernel(x16_ref, x32_ref, k_ref, v_ref, wq_ref, wo_ref, g_ref, b_ref, o32_ref, o16_ref):
    q = _dot(x16_ref[0], wq_ref[...]).astype(BF16)
    outs = []
    for h in range(X_HEADS):
        sl = slice(h * X_DH, (h + 1) * X_DH)
        lg = _dot_nt(q[:, sl], k_ref[0, :, sl]) * (X_DH ** -0.5)
        mx = jnp.max(lg, axis=1, keepdims=True)
        e = jnp.exp(lg - mx)
        p = e / jnp.sum(e, axis=1, keepdims=True)
        outs.append(_dot(p.astype(BF16), v_ref[0, :, sl]))
    o = jnp.concatenate(outs, axis=1).astype(BF16)
    out = _ln(ALPHA * x32_ref[0] + _dot(o, wo_ref[...]), g_ref[...], b_ref[...])
    o32_ref[0] = out
    o16_ref[0] = out.astype(BF16)


def _xattn(x16, x32, kk, vv, wq, wo, g, b, tm):
    bsz, s, d = x32.shape
    tm = min(tm, s)
    m = kk.shape[1]
    full = lambda bi, i: (0, 0)
    tok = lambda bi, i: (bi, i, 0)
    mem = lambda bi, i: (bi, 0, 0)
    return pl.pallas_call(
        _xattn_kernel,
        grid=(bsz, s // tm),
        in_specs=[pl.BlockSpec((1, tm, d), tok), pl.BlockSpec((1, tm, d), tok),
                  pl.BlockSpec((1, m, d), mem), pl.BlockSpec((1, m, d), mem),
                  pl.BlockSpec((d, d), full), pl.BlockSpec((d, d), full),
                  pl.BlockSpec((1, d), full), pl.BlockSpec((1, d), full)],
        out_specs=[pl.BlockSpec((1, tm, d), tok), pl.BlockSpec((1, tm, d), tok)],
        out_shape=[jax.ShapeDtypeStruct((bsz, s, d), F32), jax.ShapeDtypeStruct((bsz, s, d), BF16)],
        compiler_params=_cp("parallel", "parallel"),
        name="xattn_ln2",
    )(x16, x32, kk, vv, wq, wo, g, b)


def _ffn_kernel(x16_ref, x32_ref, wg_ref, wu_ref, wd_ref, g_ref, b_ref, o32_ref, o16_ref, acc_sc):
    j = pl.program_id(1)

    @pl.when(j == 0)
    def _():
        acc_sc[...] = jnp.zeros(acc_sc.shape, F32)

    x = x16_ref[...]
    act = _silu(_dot(x, wg_ref[...])) * _dot(x, wu_ref[...])
    acc_sc[...] += _dot(act.astype(BF16), wd_ref[...])

    @pl.when(j == pl.num_programs(1) - 1)
    def _():
        out = _ln(ALPHA * x32_ref[...] + acc_sc[...], g_ref[...], b_ref[...])
        o32_ref[...] = out
        o16_ref[...] = out.astype(BF16)


def _ffn(x16, x32, wgu, wd, g, b, tm, bf):
    t, d = x32.shape
    tm = min(tm, t)
    dff = wd.shape[0]
    nf = dff // bf
    tok = lambda i, j: (i, 0)
    full = lambda i, j: (0, 0)
    return pl.pallas_call(
        _ffn_kernel,
        grid=(t // tm, nf),
        in_specs=[pl.BlockSpec((tm, d), tok), pl.BlockSpec((tm, d), tok),
                  pl.BlockSpec((d, bf), lambda i, j: (0, j)),
                  pl.BlockSpec((d, bf), lambda i, j: (0, j + nf)),
                  pl.BlockSpec((bf, d), lambda i, j: (j, 0)),
                  pl.BlockSpec((1, d), full), pl.BlockSpec((1, d), full)],
        out_specs=[pl.BlockSpec((tm, d), tok), pl.BlockSpec((tm, d), tok)],
        out_shape=[jax.ShapeDtypeStruct((t, d), F32), jax.ShapeDtypeStruct((t, d), BF16)],
        scratch_shapes=[pltpu.VMEM((tm, d), F32)],
        compiler_params=_cp("parallel", "arbitrary"),
        name="ffn_ln3",
    )(x16, x32, wgu, wgu, wd, g, b)


def _moe_kernel(x16_ref, x32_ref, rw_ref, rb_ref, wg_ref, wu_ref, wd_ref, g_ref, b_ref, o_ref, acc_sc, comb_sc):
    e = pl.program_id(1)
    j = pl.program_id(2)
    first = jnp.logical_and(e == 0, j == 0)
    last = jnp.logical_and(e == pl.num_programs(1) - 1, j == pl.num_programs(2) - 1)

    @pl.when(first)
    def _():
        acc_sc[...] = jnp.zeros(acc_sc.shape, F32)
        lane = lax.broadcasted_iota(jnp.int32, comb_sc.shape, 1).astype(F32)
        logits = _dot_x3(x32_ref[...], rw_ref[...]) + rb_ref[...]
        logits = jnp.where(lane < N_EXPERTS, logits, -jnp.inf)
        m1 = jnp.max(logits, axis=1, keepdims=True)
        i1 = jnp.min(jnp.where(logits == m1, lane, 128.0), axis=1, keepdims=True)
        rest = jnp.where(lane == i1, -jnp.inf, logits)
        m2 = jnp.max(rest, axis=1, keepdims=True)
        i2 = jnp.min(jnp.where(rest == m2, lane, 128.0), axis=1, keepdims=True)
        e2 = jnp.exp(m2 - m1)
        w1 = 1.0 / (1.0 + e2)
        w2 = e2 / (1.0 + e2)
        comb_sc[...] = jnp.where(lane == i1, w1, 0.0) + jnp.where(lane == i2, w2, 0.0)

    lane = lax.broadcasted_iota(jnp.int32, comb_sc.shape, 1)
    ce = jnp.sum(jnp.where(lane == e, comb_sc[...], 0.0), axis=1, keepdims=True)
    x = x16_ref[...]
    act = _silu(_dot(x, wg_ref[0])) * _dot(x, wu_ref[0])
    acc_sc[...] += ce * _dot(act.astype(BF16), wd_ref[0])

    @pl.when(last)
    def _():
        o_ref[...] = _ln(ALPHA * x32_ref[...] + acc_sc[...], g_ref[...], b_ref[...])


def _moe(x16, x32, rw, rb, wgu, wd, g, b, tm, bf):
    t, d = x32.shape
    tm = min(tm, t)
    ne, dff, _ = wd.shape
    nf = dff // bf
    tok = lambda i, e, j: (i, 0)
    full = lambda i, e, j: (0, 0)
    return pl.pallas_call(
        _moe_kernel,
        grid=(t // tm, ne, nf),
        in_specs=[pl.BlockSpec((tm, d), tok), pl.BlockSpec((tm, d), tok),
                  pl.BlockSpec((d, 128), full), pl.BlockSpec((1, 128), full),
                  pl.BlockSpec((1, d, bf), lambda i, e, j: (e, 0, j)),
                  pl.BlockSpec((1, d, bf), lambda i, e, j: (e, 0, j + nf)),
                  pl.BlockSpec((1, bf, d), lambda i, e, j: (e, j, 0)),
                  pl.BlockSpec((1, d), full), pl.BlockSpec((1, d), full)],
        out_specs=pl.BlockSpec((tm, d), tok),
        out_shape=jax.ShapeDtypeStruct((t, d), F32),
        scratch_shapes=[pltpu.VMEM((tm, d), F32), pltpu.VMEM((tm, 128), F32)],
        compiler_params=_cp("parallel", "arbitrary", "arbitrary"),
        name="moe_ln3",
    )(x16, x32, rw, rb, wgu, wgu, wd, g, b)


def _src_layout(with_vres):
    cols = [('m_qk', 512), ('m_v', 512), ('m_o', 512), ('m_i', 4), ('m_f', 4),
            ('f_q', 512), ('f_k', 512), ('f_v', 512), ('f_f', 8), ('gate', 3072),
            ('r_r', 512), ('r_k', 512), ('r_v', 512), ('r_w', 64), ('r_a', 64), ('r_g', 128)]
    if with_vres:
        cols.append(('r_vres', 32))
    layout, start = {}, 0
    for name, width in cols:
        layout[name] = (start, start + width)
        start += width
    return layout


def _relayout_cols(w, with_vres):
    lay = _src_layout(with_vres)
    lead = w.shape[:-1]
    z = lambda n: jnp.zeros(lead + (n,), w.dtype)
    c = lambda name: w[..., lay[name][0]:lay[name][1]]

    def slots(x, scale):
        x = x.reshape(lead + (F_HEADS, F_DH)) * scale
        return jnp.concatenate([x, jnp.zeros_like(x)], axis=-1).reshape(lead + (F_HEADS * F_PAD,))

    parts = [c('gate'), c('r_r'), c('r_k'), c('r_v'),
             c('r_w'), c('r_a'), c('r_g'),
             c('m_i'), c('m_f'), c('f_f'), z(16), c('r_vres') if with_vres else z(32), z(64), z(128),
             slots(c('f_q'), F_DH ** -0.5 * LOG2E), slots(c('f_k'), 1.0),
             c('m_qk'), c('m_v'), c('m_o'), c('f_v')]
    return jnp.concatenate(parts, axis=-1)


def _row(v):
    return v.reshape(1, -1).astype(F32)


def _branches(x16, bsz, s, prm, v_first):
    (w_in, b_in, m_conv, m_norm, m_up, f_up, r_mu, r_wbias, r_wB, r_abias, r_aB, r_vbias, r_vB,
     r_gB, r_kk, r_ka, r_rk, r_ln_g, r_ln_b, r_up, w_out, ln1_g, ln1_b) = prm
    with_vres = r_vbias is not None
    w_all = _relayout_cols(w_in, with_vres)
    b_all = _relayout_cols(b_in, with_vres).reshape(1, NP)
    p16 = _matmul_bias(x16, w_all.astype(BF16), b_all, BF16, 1024, 1024, "in_proj")
    gates = _matmul_bias(x16, w_all[:, C_SMALL:C_SMALL + 128].astype(BF16), b_all[:, C_SMALL:C_SMALL + 128],
                         F32, 2048, 128, "in_proj_gates")
    p16 = p16.reshape(bsz, s, NP)
    gates = gates.reshape(bsz, s, 128)
    gates_t = jnp.transpose(gates, (0, 2, 1))

    h_a = _mlstm(p16, gates, gates_t, m_conv.astype(F32), _row(m_norm), 256)
    q_aug, k_aug, v_aug = _fox_prep(p16, gates_t, 512)
    h_b = _fox_attention(q_aug, k_aug, v_aug, 512)
    lay = _src_layout(with_vres)
    r0 = lay['r_r'][0]
    mu = r_mu[:1536].reshape(1, 1536)
    mum = jnp.concatenate([r_mu[1536:1792], jnp.zeros((32,), F32),
                           r_mu[lay['r_vres'][0] - r0:] if with_vres else jnp.zeros((32,), F32),
                           jnp.zeros((192,), F32)]).reshape(1, 512)
    z = jnp.zeros((64, 512), F32)
    w_wa = jnp.concatenate([jnp.concatenate([r_wB, z], axis=1), jnp.concatenate([z, r_aB], axis=1)], axis=0).astype(BF16)
    w_v = None
    if with_vres:
        w_v = jnp.concatenate([jnp.zeros((32, 512), F32), r_vB, jnp.zeros((64, 512), F32)], axis=0).astype(BF16)
    hid = jnp.arange(512) // R_DH
    headsum = (hid[:, None] == hid[None, :]).astype(BF16)
    outs = _rwkv_prep(p16, v_first, mu, mum, w_wa, w_v, r_gB.astype(BF16), headsum, _row(r_wbias), _row(r_abias),
                      _row(r_vbias) if with_vres else None, _row(r_kk), _row(r_ka), _row(r_rk), 256)
    r, lw, k2, v, kk, b, g, bv = outs[:8]
    v_own = v_first if with_vres else outs[8]
    m_c, n_c, rq, y0 = _rwkv_chunks(r, lw, k2, v, kk, b, 2)
    y = _rwkv_scan(m_c, n_c, rq, y0, bv, g, _row(r_ln_g), _row(r_ln_b), 4)
    return p16, h_a, h_b, y, v_own


def _token_mixer(x16, x32, bsz, s, prm, v_first):
    m_up, f_up, r_up, w_out, ln1_g, ln1_b = prm[4], prm[5], prm[19], prm[20], prm[21], prm[22]
    t = bsz * s
    p16, h_a, h_b, y, v_own = _branches(x16, bsz, s, prm, v_first)
    x1_32, x1_16 = _merge(h_a.reshape(t, 512), h_b.reshape(t, 512), y.reshape(t, 512), p16.reshape(t, NP), x32,
                          m_up.astype(BF16), f_up.astype(BF16), r_up.astype(BF16), w_out.astype(BF16),
                          _row(ln1_g), _row(ln1_b), 512)
    return x1_32, x1_16, v_own


def kernel(x, mem, mem_ln_g, mem_ln_b, w_in_0, b_in_0, m_conv_0, m_norm_0, m_up_0, f_up_0, r_mu_0, r_wbias_0, r_wB_0, r_abias_0, r_aB_0, r_gB_0, r_kk_0, r_ka_0, r_rk_0, r_ln_g_0, r_ln_b_0, r_up_0, w_out_0, ln1_g_0, ln1_b_0, x_wq_0, x_wkv_0, x_wo_0, ln2_g_0, ln2_b_0, ff_wgu_0, ff_wd_0, ln3_g_0, ln3_b_0, w_in_1, b_in_1, m_conv_1, m_norm_1, m_up_1, f_up_1, r_mu_1, r_wbias_1, r_wB_1, r_abias_1, r_aB_1, r_vbias_1, r_vB_1, r_gB_1, r_kk_1, r_ka_1, r_rk_1, r_ln_g_1, r_ln_b_1, r_up_1, w_out_1, ln1_g_1, ln1_b_1, x_wq_1, x_wkv_1, x_wo_1, ln2_g_1, ln2_b_1, ex_router_1, ex_router_b_1, ex_wgu_1, ex_wd_1, ln3_g_1, ln3_b_1):
    bsz, s, d = x.shape
    t = bsz * s
    x32 = x.reshape(t, d)
    x16 = x32.astype(BF16)

    mixers = (
        (w_in_0, b_in_0, m_conv_0, m_norm_0, m_up_0, f_up_0, r_mu_0, r_wbias_0, r_wB_0, r_abias_0, r_aB_0, None, None,
         r_gB_0, r_kk_0, r_ka_0, r_rk_0, r_ln_g_0, r_ln_b_0, r_up_0, w_out_0, ln1_g_0, ln1_b_0),
        (w_in_1, b_in_1, m_conv_1, m_norm_1, m_up_1, f_up_1, r_mu_1, r_wbias_1, r_wB_1, r_abias_1, r_aB_1, r_vbias_1, r_vB_1,
         r_gB_1, r_kk_1, r_ka_1, r_rk_1, r_ln_g_1, r_ln_b_1, r_up_1, w_out_1, ln1_g_1, ln1_b_1),
    )
    xattn = ((x_wq_0, x_wo_0, ln2_g_0, ln2_b_0), (x_wq_1, x_wo_1, ln2_g_1, ln2_b_1))
    kv0, kv1 = _mem_kv(mem, _row(mem_ln_g), _row(mem_ln_b), x_wkv_0.astype(BF16), x_wkv_1.astype(BF16))
    kvs = (kv0, kv1)

    v_first = None
    for l in range(DEPTH):
        x32, x16, v_own = _token_mixer(x16, x32, bsz, s, mixers[l], v_first)
        if l == 0:
            v_first = v_own
        wq, wo, g2, b2 = xattn[l]
        kv = kvs[l]
        x32, x16 = _xattn(x16.reshape(bsz, s, d), x32.reshape(bsz, s, d), kv[:, :, :d], kv[:, :, d:],
                          wq.astype(BF16), wo.astype(BF16), _row(g2), _row(b2), 512)
        x32, x16 = x32.reshape(t, d), x16.reshape(t, d)
        if l % 2 == 0:
            x32, x16 = _ffn(x16, x32, ff_wgu_0.astype(BF16), ff_wd_0.astype(BF16), _row(ln3_g_0), _row(ln3_b_0), 512, 1408)
        else:
            rw = jnp.concatenate([ex_router_1, jnp.zeros((d, 128 - N_EXPERTS), F32)], axis=1)
            rb = jnp.concatenate([ex_router_b_1, jnp.zeros((128 - N_EXPERTS,), F32)]).reshape(1, 128)
            x32 = _moe(x16, x32, rw, rb, ex_wgu_1.astype(BF16), ex_wd_1.astype(BF16), _row(ln3_g_1), _row(ln3_b_1), 1024, 896)
    return x32.reshape(bsz, s, d)
```

```python
import functools

import jax
import jax.numpy as jnp
from jax import lax
from jax.experimental import pallas as pl
from jax.experimental.pallas import tpu as pltpu

F32 = jnp.float32
BF16 = jnp.bfloat16

D_MODEL = 1024
DEPTH = 2
M_HEADS, M_DV, M_DQK, M_CONV, M_CHUNK = 4, 128, 64, 4, 64
F_HEADS, F_DH, F_PAD = 8, 64, 128
R_HEADS, R_DH, R_CHUNK = 8, 64, 64
R_GN_EPS = 64e-5
X_HEADS, X_DH = 4, 256
N_EXPERTS = 8
ALPHA = (2.0 * DEPTH) ** 0.25
LN_EPS = 1e-5
HEAD_NORM_EPS = 1e-6
NEG = -1e30
LOG2E = 1.4426950408889634

C_GATE = 0
C_RKV = 3072
C_RMISC = 4608
C_SMALL = 4864
C_FQ = 5120
C_FK = 6144
C_MQK = 7168
C_MV = 7680
C_MO = 8192
C_FV = 8704
NP = 9216
VMEM_LIMIT = 56 * 1024 * 1024


def _cp(*sem):
    return pltpu.CompilerParams(dimension_semantics=sem, vmem_limit_bytes=VMEM_LIMIT)


def _dot(a, b):
    return jnp.dot(a, b, preferred_element_type=F32)


def _dot_nt(a, b):
    return lax.dot_general(a, b, (((1,), (1,)), ((), ())), preferred_element_type=F32)


def _dot_tn(a, b):
    return lax.dot_general(a, b, (((0,), (0,)), ((), ())), preferred_element_type=F32)


def _split(x):
    hi = x.astype(BF16)
    lo = (x - hi.astype(F32)).astype(BF16)
    return hi, lo


def _split3(x):
    h1 = x.astype(BF16)
    r1 = x - h1.astype(F32)
    h2 = r1.astype(BF16)
    h3 = (r1 - h2.astype(F32)).astype(BF16)
    return h1, h2, h3


def _dot_x2(a, b16):
    hi, lo = _split(a)
    return _dot(hi, b16) + _dot(lo, b16)


def _dot_x3(a, b):
    ah, al = _split(a)
    bh, bl = _split(b)
    return _dot(ah, bh) + (_dot(al, bh) + _dot(ah, bl))


def _dot16(a, b):
    return _dot(a.astype(BF16), b.astype(BF16))


def _sigmoid(x):
    return 1.0 / (1.0 + jnp.exp(-x))


def _log_sigmoid(x):
    return jnp.minimum(x, 0.0) - jnp.log(1.0 + jnp.exp(-jnp.abs(x)))


def _softplus(x):
    return jnp.maximum(x, 0.0) + jnp.log(1.0 + jnp.exp(-jnp.abs(x)))


def _silu(x):
    return x * _sigmoid(x)


def _ln(z, g, b, eps=LN_EPS):
    mu = jnp.mean(z, axis=-1, keepdims=True)
    zc = z - mu
    var = jnp.mean(zc * zc, axis=-1, keepdims=True)
    return zc * lax.rsqrt(var + eps) * g + b


def _mm_kernel(a_ref, w_ref, b_ref, o_ref):
    o_ref[...] = (_dot(a_ref[...], w_ref[...]) + b_ref[...]).astype(o_ref.dtype)


def _matmul_bias(a, w, b, out_dtype, tm, tn, name):
    m, k = a.shape
    n = w.shape[1]
    tm, tn = min(tm, m), min(tn, n)
    return pl.pallas_call(
        _mm_kernel,
        grid=(m // tm, n // tn),
        in_specs=[pl.BlockSpec((tm, k), lambda i, j: (i, 0)),
                  pl.BlockSpec((k, tn), lambda i, j: (0, j)),
                  pl.BlockSpec((1, tn), lambda i, j: (0, j))],
        out_specs=pl.BlockSpec((tm, tn), lambda i, j: (i, j)),
        out_shape=jax.ShapeDtypeStruct((m, n), out_dtype),
        compiler_params=_cp("parallel", "parallel"),
        name=name,
    )(a, w, b)


def _memkv_kernel(mem_ref, g_ref, b_ref, w0_ref, w1_ref, o0_ref, o1_ref):
    mn = _ln(mem_ref[0], g_ref[...], b_ref[...]).astype(BF16)
    o0_ref[0] = _dot(mn, w0_ref[...]).astype(BF16)
    o1_ref[0] = _dot(mn, w1_ref[...]).astype(BF16)


def _mem_kv(mem, g, b, wkv0, wkv1):
    bsz, m, d = mem.shape
    n = wkv0.shape[1]
    full = lambda i: (0, 0)
    return pl.pallas_call(
        _memkv_kernel,
        grid=(bsz,),
        in_specs=[pl.BlockSpec((1, m, d), lambda i: (i, 0, 0)),
                  pl.BlockSpec((1, d), full), pl.BlockSpec((1, d), full),
                  pl.BlockSpec((d, n), full), pl.BlockSpec((d, n), full)],
        out_specs=[pl.BlockSpec((1, m, n), lambda i: (i, 0, 0))] * 2,
        out_shape=[jax.ShapeDtypeStruct((bsz, m, n), BF16)] * 2,
        compiler_params=_cp("parallel"),
        name="mem_kv",
    )(mem, g, b, wkv0, wkv1)


def _foxprep_kernel(g_ref, q_ref, k_ref, v_ref, qo_ref, ko_ref, vo_ref, carry_sc, *, tt):
    blk = 128

    @pl.when(pl.program_id(1) == 0)
    def _():
        carry_sc[...] = jnp.zeros(carry_sc.shape, F32)

    r = lax.broadcasted_iota(jnp.int32, (blk, blk), 0)
    c = lax.broadcasted_iota(jnp.int32, (blk, blk), 1)
    upper = (r <= c).astype(BF16)
    carry = carry_sc[:, 0:1]
    parts = []
    for i in range(tt // blk):
        ls = _log_sigmoid(g_ref[0, :, i * blk:(i + 1) * blk])
        h1, h2, h3 = _split3(ls)
        cs = _dot(h1, upper) + (_dot(h2, upper) + _dot(h3, upper)) + carry
        carry = cs[:, blk - 1:blk]
        parts.append(cs)
    carry_sc[...] = jnp.broadcast_to(carry, carry_sc.shape)
    cum = jnp.concatenate(parts, axis=1) * LOG2E
    c1, c2, c3 = [p.astype(F32) for p in _split3(cum)]
    w = F_HEADS * F_PAD
    row = lax.broadcasted_iota(jnp.int32, (F_HEADS, w), 0)
    lane = lax.broadcasted_iota(jnp.int32, (F_HEADS, w), 1)
    sel = lambda j: (lane == row * F_PAD + F_DH + j).astype(F32)
    lane1 = lax.broadcasted_iota(jnp.int32, (1, w), 1) % F_PAD
    ones_q = jnp.logical_and(lane1 >= F_DH + 3, lane1 < F_DH + 6).astype(F32)
    ones_k = jnp.logical_and(lane1 >= F_DH, lane1 < F_DH + 3).astype(F32)
    aug_q = _dot_tn(c1, sel(0)) + _dot_tn(c2, sel(1)) + _dot_tn(c3, sel(2)) + ones_q
    aug_k = ones_k - (_dot_tn(c1, sel(3)) + _dot_tn(c2, sel(4)) + _dot_tn(c3, sel(5)))
    qo_ref[0] = (q_ref[0].astype(F32) + aug_q).astype(BF16)
    ko_ref[0] = (k_ref[0].astype(F32) + aug_k).astype(BF16)
    wv = F_HEADS * F_DH
    src = lax.broadcasted_iota(jnp.int32, (wv, w), 0)
    dst = lax.broadcasted_iota(jnp.int32, (wv, w), 1)
    place = (dst == (src // F_DH) * F_PAD + src % F_DH).astype(BF16)
    ones_v = (lane1 >= F_DH).astype(F32)
    vo_ref[0] = jnp.transpose(_dot(v_ref[0], place) + ones_v).astype(BF16)


def _fox_prep(p16, gates_t, tt):
    bsz, s, _ = p16.shape
    tt = min(tt, s)
    w = F_HEADS * F_PAD
    wv = F_HEADS * F_DH
    tok = lambda b, i: (b, i, 0)
    return pl.pallas_call(
        functools.partial(_foxprep_kernel, tt=tt),
        grid=(bsz, s // tt),
        in_specs=[pl.BlockSpec((1, 8, tt), lambda b, i: (b, 1, i)),
                  pl.BlockSpec((1, tt, w), lambda b, i: (b, i, C_FQ // w)),
                  pl.BlockSpec((1, tt, w), lambda b, i: (b, i, C_FK // w)),
                  pl.BlockSpec((1, tt, wv), lambda b, i: (b, i, C_FV // wv))],
        out_specs=[pl.BlockSpec((1, tt, w), tok), pl.BlockSpec((1, tt, w), tok),
                   pl.BlockSpec((1, w, tt), lambda b, i: (b, 0, i))],
        out_shape=[jax.ShapeDtypeStruct((bsz, s, w), BF16), jax.ShapeDtypeStruct((bsz, s, w), BF16),
                   jax.ShapeDtypeStruct((bsz, w, s), BF16)],
        scratch_shapes=[pltpu.VMEM((8, 128), F32)],
        compiler_params=_cp("parallel", "arbitrary"),
        name="fox_prep",
    )(gates_t, p16, p16, p16)


def _fox_kernel(qt_ref, kt_ref, q_ref, k_ref, v_ref, o_ref, m_sc, acc_sc, *, t):
    qi = qt_ref[pl.program_id(1)]
    ki = kt_ref[pl.program_id(1)]

    @pl.when(ki == 0)
    def _():
        m_sc[...] = jnp.full(m_sc.shape, NEG, F32)
        acc_sc[...] = jnp.zeros(acc_sc.shape, F32)

    def block(diag):
        if diag:
            key = lax.broadcasted_iota(jnp.int32, (t, t), 0)
            qry = lax.broadcasted_iota(jnp.int32, (t, t), 1)
            mask = key <= qry
        pss = [slice(h * F_PAD, (h + 1) * F_PAD) for h in range(F_HEADS)]
        scores = lambda h: _dot_nt(k_ref[0, :, pss[h]], q_ref[0, :, pss[h]])
        s_next = scores(0)
        for h in range(F_HEADS):
            s = s_next
            if h + 1 < F_HEADS:
                s_next = scores(h + 1)
            if diag:
                s = jnp.where(mask, s, NEG)
            m_old = m_sc[h]
            m_new = jnp.maximum(m_old, jnp.max(s, axis=0, keepdims=True))
            p = jnp.exp2(s - m_new).astype(BF16)
            acc_sc[h] = jnp.exp2(m_old - m_new) * acc_sc[h] + _dot(v_ref[0, pss[h], :], p)
            m_sc[h] = m_new

    @pl.when(ki < qi)
    def _():
        block(False)

    @pl.when(ki == qi)
    def _():
        block(True)
        for h in range(F_HEADS):
            acc = acc_sc[h]
            out_t = acc[:F_DH, :] / acc[F_DH:F_DH + 1, :]
            o_ref[0, :, h * F_DH:(h + 1) * F_DH] = jnp.transpose(out_t).astype(o_ref.dtype)


def _fox_attention(q_aug, k_aug, v_aug_t, t):
    bsz, s, w = q_aug.shape
    t = min(t, s)
    n = s // t
    wv = F_HEADS * F_DH
    pairs = [(i, j) for i in range(n) for j in range(i + 1)]
    qt = jnp.asarray([p[0] for p in pairs], jnp.int32)
    kt = jnp.asarray([p[1] for p in pairs], jnp.int32)
    return pl.pallas_call(
        functools.partial(_fox_kernel, t=t),
        grid_spec=pltpu.PrefetchScalarGridSpec(
            num_scalar_prefetch=2,
            grid=(bsz, len(pairs)),
            in_specs=[pl.BlockSpec((1, t, w), lambda b, p, qt, kt: (b, qt[p], 0)),
                      pl.BlockSpec((1, t, w), lambda b, p, qt, kt: (b, kt[p], 0)),
                      pl.BlockSpec((1, w, t), lambda b, p, qt, kt: (b, 0, kt[p]))],
            out_specs=pl.BlockSpec((1, t, wv), lambda b, p, qt, kt: (b, qt[p], 0)),
            scratch_shapes=[pltpu.VMEM((F_HEADS, 1, t), F32), pltpu.VMEM((F_HEADS, F_PAD, t), F32)]),
        out_shape=jax.ShapeDtypeStruct((bsz, s, wv), BF16),
        compiler_params=_cp("parallel", "arbitrary"),
        name="fox_attention",
    )(qt, kt, q_aug, k_aug, v_aug_t)


def _mlstm_kernel(qk_ref, v_ref, o_ref, g_ref, gt_ref, cw_ref, nw_ref, out_ref,
                  prev_sc, q_sc, k_sc, c_sc, m_sc, *, lt):
    ti = pl.program_id(1)
    L = M_CHUNK
    H = range(M_HEADS)

    @pl.when(ti == 0)
    def _():
        prev_sc[...] = jnp.zeros(prev_sc.shape, F32)
        c_sc[...] = jnp.zeros(c_sc.shape, F32)
        m_sc[...] = jnp.zeros(m_sc.shape, F32)

    u = qk_ref[0].astype(F32)
    prev = prev_sc[...]
    rows = lax.broadcasted_iota(jnp.int32, u.shape, 0)
    acc = u * cw_ref[M_CONV - 1:M_CONV, :]
    for d in range(1, M_CONV):
        sh = jnp.where(rows < d, pltpu.roll(prev, d, 0), pltpu.roll(u, d, 0))
        acc = acc + sh * cw_ref[M_CONV - 1 - d:M_CONV - d, :]
    prev_sc[...] = u
    qk = _silu(acc)
    nq = M_HEADS * M_DQK
    q_sc[...] = qk[:, :nq].astype(BF16)
    k_sc[...] = qk[:, nq:] * (M_DQK ** -0.5)

    r = lax.broadcasted_iota(jnp.int32, (L, L), 0)
    c = lax.broadcasted_iota(jnp.int32, (L, L), 1)
    tri = c <= r
    lower = tri.astype(BF16)
    upper = (r <= c).astype(BF16)
    ones = jnp.ones((L, M_DV), BF16)

    for ch in range(lt // L):
        rs = slice(ch * L, (ch + 1) * L)
        gc = g_ref[0, rs, :]
        gr = gt_ref[0, :, rs]
        lc1, lc2 = _split(_log_sigmoid(gc))
        lr1, lr2 = _split(_log_sigmoid(gr))
        bcol = _dot(lower, lc1) + _dot(lower, lc2)
        brow = _dot(lr1, upper) + _dot(lr2, upper)
        hsl = [slice(h * M_DV, (h + 1) * M_DV) for h in H]
        qsl = [slice(h * M_DQK, (h + 1) * M_DQK) for h in H]
        bc = [bcol[:, 4 + h:5 + h] for h in H]
        m_prev = [m_sc[h:h + 1, 0:1] for h in H]
        qh = [q_sc[rs, qsl[h]] for h in H]
        kh = [k_sc[rs, qsl[h]] for h in H]
        vext = [jnp.concatenate([v_ref[0, rs, hsl[h]], ones], axis=1) for h in H]
        cext = [c_sc[h] for h in H]
        dm = [jnp.where(tri, bc[h] - brow[4 + h:5 + h, :] + gr[h:h + 1, :], -jnp.inf) for h in H]
        d_inter = [bc[h] + m_prev[h] for h in H]
        m_t = [jnp.maximum(d_inter[h], jnp.max(dm[h], axis=1, keepdims=True)) for h in H]
        qk_s = [_dot_nt(qh[h], kh[h].astype(BF16)) for h in H]
        qc = [_dot(qh[h], cext[h].astype(BF16)) for h in H]
        w_intra = [(qk_s[h] * jnp.exp(dm[h] - m_t[h])).astype(BF16) for h in H]
        ext = [jnp.exp(d_inter[h] - m_t[h]) * qc[h] + _dot(w_intra[h], vext[h]) for h in H]
        b_last = [bc[h][L - 1:L, :] for h in H]
        g = [b_last[h] - bc[h] + gc[:, h:h + 1] for h in H]
        m_new = [jnp.maximum(b_last[h] + m_prev[h], jnp.max(g[h], axis=0, keepdims=True)) for h in H]
        wk = [(jnp.exp(g[h] - m_new[h]) * kh[h]).astype(BF16) for h in H]
        upd = [_dot_tn(wk[h], vext[h]) for h in H]
        for h in H:
            c_sc[h] = jnp.exp(b_last[h] + m_prev[h] - m_new[h]) * cext[h] + upd[h]
            m_sc[h:h + 1, :] = jnp.broadcast_to(m_new[h], (1, m_sc.shape[1]))
        for h in H:
            hh = ext[h][:, :M_DV] / jnp.maximum(jnp.abs(ext[h][:, M_DV:M_DV + 1]), jnp.exp(-m_t[h]))
            mu = jnp.mean(hh, axis=1, keepdims=True)
            hc = hh - mu
            var = jnp.mean(hc * hc, axis=1, keepdims=True)
            hn = hc * lax.rsqrt(var + HEAD_NORM_EPS) * nw_ref[:, hsl[h]]
            out_ref[0, rs, hsl[h]] = (hn * _sigmoid(o_ref[0, rs, hsl[h]].astype(F32))).astype(out_ref.dtype)


def _mlstm(p16, gates, gates_t, conv_w, norm_w, lt):
    bsz, s, _ = p16.shape
    lt = min(lt, s)
    w = M_HEADS * M_DV
    full = lambda b, i: (0, 0)
    return pl.pallas_call(
        functools.partial(_mlstm_kernel, lt=lt),
        grid=(bsz, s // lt),
        in_specs=[pl.BlockSpec((1, lt, w), lambda b, i: (b, i, C_MQK // w)),
                  pl.BlockSpec((1, lt, w), lambda b, i: (b, i, C_MV // w)),
                  pl.BlockSpec((1, lt, w), lambda b, i: (b, i, C_MO // w)),
                  pl.BlockSpec((1, lt, 128), lambda b, i: (b, i, 0)),
                  pl.BlockSpec((1, 8, lt), lambda b, i: (b, 0, i)),
                  pl.BlockSpec((M_CONV, w), full),
                  pl.BlockSpec((1, w), full)],
        out_specs=pl.BlockSpec((1, lt, w), lambda b, i: (b, i, 0)),
        out_shape=jax.ShapeDtypeStruct((bsz, s, w), BF16),
        scratch_shapes=[pltpu.VMEM((lt, w), F32),
                        pltpu.VMEM((lt, M_HEADS * M_DQK), BF16), pltpu.VMEM((lt, M_HEADS * M_DQK), F32),
                        pltpu.VMEM((M_HEADS, M_DQK, 2 * M_DV), F32), pltpu.VMEM((8, 128), F32)],
        compiler_params=_cp("parallel", "arbitrary"),
        name="mlstm",
    )(p16, p16, p16, gates, gates_t, conv_w, norm_w)


def _rprep_kernel(*refs, tt, has_vres):
    if has_vres:
        (p_ref, pm_ref, vf_ref, mu_ref, mum_ref, wa_ref, wv_ref, wg_ref, hs_ref, wbias_ref, abias_ref, vbias_ref,
         kkw_ref, kaw_ref, rkw_ref,
         r_o, lw_o, k_o, v_o, kk_o, b_o, g_o, bv_o, carry_sc, carrym_sc) = refs
    else:
        (p_ref, pm_ref, mu_ref, mum_ref, wa_ref, wg_ref, hs_ref, wbias_ref, abias_ref,
         kkw_ref, kaw_ref, rkw_ref,
         r_o, lw_o, k_o, v_o, kk_o, b_o, g_o, bv_o, vown_o, carry_sc, carrym_sc) = refs
    ti = pl.program_id(1)

    @pl.when(ti == 0)
    def _():
        carry_sc[...] = jnp.zeros(carry_sc.shape, F32)
        carrym_sc[...] = jnp.zeros(carrym_sc.shape, F32)

    rows = lax.broadcasted_iota(jnp.int32, (tt, 512), 0)

    def shifted(ref, cref, mref, c0, width):
        x = ref[0, :, c0:c0 + width].astype(F32)
        prev = jnp.where(rows[:, :width] == 0, cref[0:1, c0:c0 + width], pltpu.roll(x, 1, 0))
        return x + mref[:, c0:c0 + width] * (prev - x)

    r = shifted(p_ref, carry_sc, mu_ref, 0, 512)
    k = shifted(p_ref, carry_sc, mu_ref, 512, 512)
    v = shifted(p_ref, carry_sc, mu_ref, 1024, 512)
    wa_in = shifted(pm_ref, carrym_sc, mum_ref, 0, 128)
    g_in = shifted(pm_ref, carrym_sc, mum_ref, 128, 128)
    lane = lax.broadcasted_iota(jnp.int32, (tt, 128), 1)
    wa_act = jnp.where(lane < 64, jnp.tanh(wa_in), wa_in).astype(BF16)
    wa = _dot(wa_act, wa_ref[...])
    w_log = -_softplus(-(wbias_ref[...] + wa[:, :512])) - 0.5
    lw_o[0] = -jnp.exp(w_log)
    a = _sigmoid(abias_ref[...] + wa[:, 512:])
    if has_vres:
        s_in = shifted(pm_ref, carrym_sc, mum_ref, 256, 128)
        mix = _sigmoid(vbias_ref[...] + _dot(s_in.astype(BF16), wv_ref[...]))
        v = v + (vf_ref[0].astype(F32) - v) * mix
    else:
        vown_o[0] = v.astype(vown_o.dtype)
    g_o[0] = _dot(_sigmoid(g_in).astype(BF16), wg_ref[...]).astype(g_o.dtype)
    hs = hs_ref[...]
    u = k * kkw_ref[...]
    nrm = jnp.sqrt(_dot_x2(u * u, hs))
    kk = u / jnp.maximum(nrm, 1e-12)
    k2 = k * (1.0 + (a - 1.0) * kaw_ref[...])
    bonus = _dot_x2(r * k2 * rkw_ref[...], hs)
    r_o[0] = r.astype(r_o.dtype)
    k_o[0] = k2.astype(k_o.dtype)
    v_o[0] = v.astype(v_o.dtype)
    kk_o[0] = kk.astype(kk_o.dtype)
    b_o[0] = (kk * a).astype(b_o.dtype)
    bv_o[0] = (bonus * v).astype(bv_o.dtype)
    carry_sc[0:1, :] = p_ref[0, tt - 1:tt, :].astype(F32)
    carrym_sc[0:1, :] = pm_ref[0, tt - 1:tt, :].astype(F32)


def _rwkv_prep(p16, v_first, mu, mum, w_wa, w_v, w_g, headsum, wbias, abias, vbias, kkw, kaw, rkw, tt):
    bsz, s, _ = p16.shape
    tt = min(tt, s)
    has_vres = v_first is not None
    full = lambda b, i: (0, 0)
    tok = lambda b, i: (b, i, 0)
    vec = pl.BlockSpec((1, 512), full)
    ins = [p16, p16]
    specs = [pl.BlockSpec((1, tt, 1536), lambda b, i: (b, i, C_RKV // 1536)),
             pl.BlockSpec((1, tt, 512), lambda b, i: (b, i, C_RMISC // 512))]
    if has_vres:
        ins.append(v_first)
        specs.append(pl.BlockSpec((1, tt, 512), tok))
    ins += [mu, mum, w_wa]
    specs += [pl.BlockSpec((1, 1536), full), pl.BlockSpec((1, 512), full), pl.BlockSpec((128, 1024), full)]
    if has_vres:
        ins.append(w_v)
        specs.append(pl.BlockSpec((128, 512), full))
    ins += [w_g, headsum, wbias, abias]
    specs += [pl.BlockSpec((128, 512), full), pl.BlockSpec((512, 512), full), vec, vec]
    if has_vres:
        ins.append(vbias)
        specs.append(vec)
    ins += [kkw, kaw, rkw]
    specs += [vec, vec, vec]
    n_out = 8 if has_vres else 9
    dts = [BF16, F32, BF16, BF16, BF16, BF16, BF16, BF16] + ([] if has_vres else [BF16])
    return pl.pallas_call(
        functools.partial(_rprep_kernel, tt=tt, has_vres=has_vres),
        grid=(bsz, s // tt),
        in_specs=specs,
        out_specs=[pl.BlockSpec((1, tt, 512), tok)] * n_out,
        out_shape=[jax.ShapeDtypeStruct((bsz, s, 512), dt) for dt in dts],
        scratch_shapes=[pltpu.VMEM((8, 1536), F32), pltpu.VMEM((8, 512), F32)],
        compiler_params=_cp("parallel", "arbitrary"),
        name="rwkv_prep",
    )(*ins)


def _rchunk_kernel(r_ref, lw_ref, k_ref, v_ref, kk_ref, b_ref, m_o, n_o, rq_o, y0_o, *, nck):
    L = R_CHUNK
    ri = lax.broadcasted_iota(jnp.int32, (L, L), 0)
    ci = lax.broadcasted_iota(jnp.int32, (L, L), 1)
    lower = (ci <= ri).astype(BF16)
    strict = ci < ri
    incl = ci <= ri
    eye = (ci == ri).astype(F32)

    kk_t, r_t, b_h, k_h, b_w, k_w, w_last, vf = [], [], [], [], [], [], [], []
    for c in range(nck):
        rs = slice(c * L, (c + 1) * L)
        lw = lw_ref[0, rs, :]
        l1, l2 = _split(lw)
        cl = _dot(lower, l1) + _dot(lower, l2)
        cl_last = cl[L - 1:L, :]
        w_inv = jnp.exp(-cl)
        w_rem = jnp.exp(cl_last - cl)
        kf = k_ref[0, rs, :].astype(F32)
        bf = b_ref[0, rs, :].astype(F32)
        kk_t.append((kk_ref[0, rs, :].astype(F32) * jnp.exp(cl - lw)).astype(BF16))
        r_t.append(r_ref[0, rs, :].astype(F32) * jnp.exp(cl))
        b_h.append((bf * w_inv).astype(BF16))
        k_h.append((kf * w_inv).astype(BF16))
        b_w.append((bf * w_rem).astype(BF16))
        k_w.append((kf * w_rem).astype(BF16))
        w_last.append(jnp.exp(cl_last))
        vf.append(v_ref[0, rs, :])

    J = [(c, slice(h * R_DH, (h + 1) * R_DH)) for c in range(nck) for h in range(R_HEADS)]
    N = range(len(J))
    a = [_dot_nt(jnp.concatenate([kk_t[c][:, sl], r_t[c][:, sl].astype(BF16)], axis=0),
                 jnp.concatenate([b_h[c][:, sl], k_h[c][:, sl]], axis=0)) for c, sl in J]
    a_ak = [jnp.where(strict, a[j][:L, L:], 0.0).astype(BF16) for j in N]
    a_rb = [jnp.where(incl, a[j][L:, :L], 0.0).astype(BF16) for j in N]
    a_rk = [jnp.where(incl, a[j][L:, L:], 0.0).astype(BF16) for j in N]
    x = [jnp.where(strict, -a[j][:L, :L], 0.0) for j in N]
    t = [eye + x[j] for j in N]
    for _ in range(5):
        x = [_dot16(x[j], x[j]) for j in N]
        t = [t[j] + _dot16(t[j], x[j]) for j in N]
    vh = [vf[c][:, sl] for c, sl in J]
    akv = [_dot(a_ak[j], vh[j]) for j in N]
    rkv = [_dot(a_rk[j], vh[j]) for j in N]
    pq16 = [_dot16(t[j], jnp.concatenate([kk_t[c][:, sl].astype(F32), akv[j]], axis=1)).astype(BF16)
            for j, (c, sl) in enumerate(J)]
    rq_y0 = [jnp.concatenate([r_t[c][:, sl], rkv[j]], axis=1) - _dot(a_rb[j], pq16[j]) for j, (c, sl) in enumerate(J)]
    pqb = [_dot_tn(pq16[j], b_w[c][:, sl]) for j, (c, sl) in enumerate(J)]
    vk = [_dot_tn(vh[j], k_w[c][:, sl]) for j, (c, sl) in enumerate(J)]
    for j, (c, sl) in enumerate(J):
        rs = slice(c * L, (c + 1) * L)
        rq_o[0, rs, sl] = rq_y0[j][:, :R_DH]
        y0_o[0, rs, sl] = rq_y0[j][:, R_DH:]
        h = j % R_HEADS
        m_o[0, c, h] = eye * w_last[c][:, sl] - pqb[j][:R_DH]
        n_o[0, c, h] = vk[j] - pqb[j][R_DH:]


def _rwkv_chunks(r, lw, k, v, kk, b, nck):
    bsz, s, w = r.shape
    L = R_CHUNK
    nc = s // L
    tok = lambda bi, c: (bi, c, 0)
    st = lambda bi, c: (bi, c, 0, 0, 0)
    sshape = (bsz, nc, R_HEADS, R_DH, R_DH)
    sblock = (1, nck, R_HEADS, R_DH, R_DH)
    return pl.pallas_call(
        functools.partial(_rchunk_kernel, nck=nck),
        grid=(bsz, nc // nck),
        in_specs=[pl.BlockSpec((1, nck * L, w), tok)] * 6,
        out_specs=[pl.BlockSpec(sblock, st), pl.BlockSpec(sblock, st),
                   pl.BlockSpec((1, nck * L, w), tok), pl.BlockSpec((1, nck * L, w), tok)],
        out_shape=[jax.ShapeDtypeStruct(sshape, F32), jax.ShapeDtypeStruct(sshape, F32),
                   jax.ShapeDtypeStruct((bsz, s, w), F32), jax.ShapeDtypeStruct((bsz, s, w), F32)],
        compiler_params=_cp("parallel", "parallel"),
        name="rwkv_chunks",
    )(r, lw, k, v, kk, b)


def _rstate_kernel(m_ref, n_ref, s_ref, s_sc, *, nck):
    H = range(R_HEADS)

    @pl.when(pl.program_id(1) == 0)
    def _():
        s_sc[...] = jnp.zeros(s_sc.shape, F32)

    for c in range(nck):
        st = [s_sc[h] for h in H]
        sp = [_split(st[h]) for h in H]
        mp = [_split(m_ref[0, c, h]) for h in H]
        snew = [_dot(sp[h][0], mp[h][0]) + (_dot(sp[h][1], mp[h][0]) + _dot(sp[h][0], mp[h][1])) for h in H]
        for h in H:
            s_ref[0, c, h] = st[h]
            s_sc[h] = snew[h] + n_ref[0, c, h]


def _rout_kernel(s_ref, rq_ref, y0_ref, bv_ref, g_ref, lng_ref, lnb_ref, o_ref, *, nck):
    L = R_CHUNK
    J = [(c, h) for c in range(nck) for h in range(R_HEADS)]
    rsl = lambda c: slice(c * L, (c + 1) * L)
    hsl = lambda h: slice(h * R_DH, (h + 1) * R_DH)
    sp = [_split(s_ref[0, c, h]) for c, h in J]
    rp = [_split(rq_ref[0, rsl(c), hsl(h)]) for c, h in J]
    y = [y0_ref[0, rsl(c), hsl(h)] + _dot_nt(rp[j][0], sp[j][0])
         + (_dot_nt(rp[j][1], sp[j][0]) + _dot_nt(rp[j][0], sp[j][1])) for j, (c, h) in enumerate(J)]
    for j, (c, h) in enumerate(J):
        rs, sl = rsl(c), hsl(h)
        mu = jnp.mean(y[j], axis=1, keepdims=True)
        yc = y[j] - mu
        var = jnp.mean(yc * yc, axis=1, keepdims=True)
        yn = yc * lax.rsqrt(var + R_GN_EPS) * lng_ref[:, sl] + lnb_ref[:, sl]
        o_ref[0, rs, sl] = ((yn + bv_ref[0, rs, sl].astype(F32)) * g_ref[0, rs, sl].astype(F32)).astype(o_ref.dtype)


def _rwkv_scan(m, n, rq, y0, bv, g, lng, lnb, nck_state, nck_out):
    bsz, s, w = rq.shape
    L = R_CHUNK
    nc = s // L
    tok = lambda bi, c: (bi, c, 0)
    st = lambda bi, c: (bi, c, 0, 0, 0)
    full = lambda bi, c: (0, 0)
    sblock = lambda k: (1, k, R_HEADS, R_DH, R_DH)
    states = pl.pallas_call(
        functools.partial(_rstate_kernel, nck=nck_state),
        grid=(bsz, nc // nck_state),
        in_specs=[pl.BlockSpec(sblock(nck_state), st), pl.BlockSpec(sblock(nck_state), st)],
        out_specs=pl.BlockSpec(sblock(nck_state), st),
        out_shape=jax.ShapeDtypeStruct(m.shape, F32),
        scratch_shapes=[pltpu.VMEM((R_HEADS, R_DH, R_DH), F32)],
        compiler_params=_cp("parallel", "arbitrary"),
        name="rwkv_state",
    )(m, n)
    k = nck_out
    return pl.pallas_call(
        functools.partial(_rout_kernel, nck=k),
        grid=(bsz, nc // k),
        in_specs=[pl.BlockSpec(sblock(k), st),
                  pl.BlockSpec((1, k * L, w), tok), pl.BlockSpec((1, k * L, w), tok),
                  pl.BlockSpec((1, k * L, w), tok), pl.BlockSpec((1, k * L, w), tok),
                  pl.BlockSpec((1, w), full), pl.BlockSpec((1, w), full)],
        out_specs=pl.BlockSpec((1, k * L, w), tok),
        out_shape=jax.ShapeDtypeStruct((bsz, s, w), BF16),
        compiler_params=_cp("parallel", "parallel"),
        name="rwkv_out",
    )(states, rq, y0, bv, g, lng, lnb)


def _merge_kernel(ha_ref, hb_ref, y_ref, gate_ref, x_ref, mup_ref, fup_ref, rup_ref, wo_ref, g_ref, b_ref,
                  o32_ref, o16_ref):
    d = D_MODEL
    merged = (_sigmoid(gate_ref[:, 0:d].astype(F32)) * _dot(ha_ref[...], mup_ref[...])
              + _sigmoid(gate_ref[:, d:2 * d].astype(F32)) * _dot(hb_ref[...], fup_ref[...])
              + _sigmoid(gate_ref[:, 2 * d:3 * d].astype(F32)) * _dot(y_ref[...], rup_ref[...]))
    mix = _dot(merged.astype(BF16), wo_ref[...])
    out = _ln(ALPHA * x_ref[...] + mix, g_ref[...], b_ref[...])
    o32_ref[...] = out
    o16_ref[...] = out.astype(BF16)


def _merge(ha, hb, y, p16, x32, mup, fup, rup, wo, g, b, tm):
    t, d = x32.shape
    tm = min(tm, t)
    full = lambda i: (0, 0)
    tok = lambda i: (i, 0)
    return pl.pallas_call(
        _merge_kernel,
        grid=(t // tm,),
        in_specs=[pl.BlockSpec((tm, 512), tok), pl.BlockSpec((tm, 512), tok), pl.BlockSpec((tm, 512), tok),
                  pl.BlockSpec((tm, 3 * d), lambda i: (i, C_GATE // (3 * d))),
                  pl.BlockSpec((tm, d), tok),
                  pl.BlockSpec((512, d), full), pl.BlockSpec((512, d), full), pl.BlockSpec((512, d), full),
                  pl.BlockSpec((d, d), full), pl.BlockSpec((1, d), full), pl.BlockSpec((1, d), full)],
        out_specs=[pl.BlockSpec((tm, d), tok), pl.BlockSpec((tm, d), tok)],
        out_shape=[jax.ShapeDtypeStruct((t, d), F32), jax.ShapeDtypeStruct((t, d), BF16)],
        compiler_params=_cp("parallel"),
        name="merge_ln1",
    )(ha, hb, y, p16, x32, mup, fup, rup, wo, g, b)


def _xattn_kernel(x16_ref, x32_ref, k_ref, v_ref, wq_ref, wo_ref, g_ref, b_ref, o32_ref, o16_ref):
    q = _dot(x16_ref[0], wq_ref[...]).astype(BF16)
    outs = []
    for h in range(X_HEADS):
        sl = slice(h * X_DH, (h + 1) * X_DH)
        lg = _dot_nt(q[:, sl], k_ref[0, :, sl]) * (X_DH ** -0.5)
        mx = jnp.max(lg, axis=1, keepdims=True)
        e = jnp.exp(lg - mx)
        p = e / jnp.sum(e, axis=1, keepdims=True)
        outs.append(_dot(p.astype(BF16), v_ref[0, :, sl]))
    o = jnp.concatenate(outs, axis=1).astype(BF16)
    out = _ln(ALPHA * x32_ref[0] + _dot(o, wo_ref[...]), g_ref[...], b_ref[...])
    o32_ref[0] = out
    o16_ref[0] = out.astype(BF16)


def _xattn(x16, x32, kk, vv, wq, wo, g, b, tm):
    bsz, s, d = x32.shape
    tm = min(tm, s)
    m = kk.shape[1]
    full = lambda bi, i: (0, 0)
    tok = lambda bi, i: (bi, i, 0)
    mem = lambda bi, i: (bi, 0, 0)
    return pl.pallas_call(
        _xattn_kernel,
        grid=(bsz, s // tm),
        in_specs=[pl.BlockSpec((1, tm, d), tok), pl.BlockSpec((1, tm, d), tok),
                  pl.BlockSpec((1, m, d), mem), pl.BlockSpec((1, m, d), mem),
                  pl.BlockSpec((d, d), full), pl.BlockSpec((d, d), full),
                  pl.BlockSpec((1, d), full), pl.BlockSpec((1, d), full)],
        out_specs=[pl.BlockSpec((1, tm, d), tok), pl.BlockSpec((1, tm, d), tok)],
        out_shape=[jax.ShapeDtypeStruct((bsz, s, d), F32), jax.ShapeDtypeStruct((bsz, s, d), BF16)],
        compiler_params=_cp("parallel", "parallel"),
        name="xattn_ln2",
    )(x16, x32, kk, vv, wq, wo, g, b)


def _ffn_kernel(x16_ref, x32_ref, wg_ref, wu_ref, wd_ref, g_ref, b_ref, o32_ref, o16_ref, acc_sc):
    j = pl.program_id(1)

    @pl.when(j == 0)
    def _():
        acc_sc[...] = jnp.zeros(acc_sc.shape, F32)

    x = x16_ref[...]
    act = _silu(_dot(x, wg_ref[...])) * _dot(x, wu_ref[...])
    acc_sc[...] += _dot(act.astype(BF16), wd_ref[...])

    @pl.when(j == pl.num_programs(1) - 1)
    def _():
        out = _ln(ALPHA * x32_ref[...] + acc_sc[...], g_ref[...], b_ref[...])
        o32_ref[...] = out
        o16_ref[...] = out.astype(BF16)


def _ffn(x16, x32, wgu, wd, g, b, tm, bf):
    t, d = x32.shape
    tm = min(tm, t)
    dff = wd.shape[0]
    nf = dff // bf
    tok = lambda i, j: (i, 0)
    full = lambda i, j: (0, 0)
    return pl.pallas_call(
        _ffn_kernel,
        grid=(t // tm, nf),
        in_specs=[pl.BlockSpec((tm, d), tok), pl.BlockSpec((tm, d), tok),
                  pl.BlockSpec((d, bf), lambda i, j: (0, j)),
                  pl.BlockSpec((d, bf), lambda i, j: (0, j + nf)),
                  pl.BlockSpec((bf, d), lambda i, j: (j, 0)),
                  pl.BlockSpec((1, d), full), pl.BlockSpec((1, d), full)],
        out_specs=[pl.BlockSpec((tm, d), tok), pl.BlockSpec((tm, d), tok)],
        out_shape=[jax.ShapeDtypeStruct((t, d), F32), jax.ShapeDtypeStruct((t, d), BF16)],
        scratch_shapes=[pltpu.VMEM((tm, d), F32)],
        compiler_params=_cp("parallel", "arbitrary"),
        name="ffn_ln3",
    )(x16, x32, wgu, wgu, wd, g, b)


RT_E1, RT_E2, RT_W1, RT_W2, RT_R1, RT_R2 = 0, 1, 2, 3, 4, 5


def _route_kernel(x_ref, rw_ref, rb_ref, info_ref, cnt_ref, carry_sc, *, tr):
    @pl.when(pl.program_id(0) == 0)
    def _():
        carry_sc[...] = jnp.zeros(carry_sc.shape, F32)

    lane = lax.broadcasted_iota(jnp.int32, (tr, 128), 1).astype(F32)
    logits = _dot_x3(x_ref[...], rw_ref[...]) + rb_ref[...]
    logits = jnp.where(lane < N_EXPERTS, logits, -jnp.inf)
    m1 = jnp.max(logits, axis=1, keepdims=True)
    i1 = jnp.min(jnp.where(logits == m1, lane, 128.0), axis=1, keepdims=True)
    rest = jnp.where(lane == i1, -jnp.inf, logits)
    m2 = jnp.max(rest, axis=1, keepdims=True)
    i2 = jnp.min(jnp.where(rest == m2, lane, 128.0), axis=1, keepdims=True)
    e2 = jnp.exp(m2 - m1)
    w1 = 1.0 / (1.0 + e2)
    w2 = e2 / (1.0 + e2)
    oh1 = lane == i1
    oh2 = lane == i2
    oh = jnp.logical_or(oh1, oh2).astype(F32)
    r = lax.broadcasted_iota(jnp.int32, (tr, tr), 0)
    c = lax.broadcasted_iota(jnp.int32, (tr, tr), 1)
    before = (c < r).astype(BF16)
    carry = carry_sc[0:1, :]
    rank = _dot(before, oh.astype(BF16)) + carry
    r1 = jnp.sum(jnp.where(oh1, rank, 0.0), axis=1, keepdims=True)
    r2 = jnp.sum(jnp.where(oh2, rank, 0.0), axis=1, keepdims=True)
    total = carry + jnp.sum(oh, axis=0, keepdims=True)
    carry_sc[...] = jnp.broadcast_to(total, carry_sc.shape)
    cnt_ref[...] = jnp.broadcast_to(total, cnt_ref.shape)
    rec = jnp.zeros((tr, 128), F32)
    for ln, val in ((RT_E1, i1), (RT_E2, i2), (RT_W1, w1), (RT_W2, w2), (RT_R1, r1), (RT_R2, r2)):
        rec = jnp.where(lane == float(ln), val, rec)
    info_ref[...] = rec


def _route(x32, rw, rb, tr):
    t, d = x32.shape
    tr = min(tr, t)
    full = lambda i: (0, 0)
    return pl.pallas_call(
        functools.partial(_route_kernel, tr=tr),
        grid=(t // tr,),
        in_specs=[pl.BlockSpec((tr, d), lambda i: (i, 0)), pl.BlockSpec((d, 128), full), pl.BlockSpec((1, 128), full)],
        out_specs=[pl.BlockSpec((tr, 128), lambda i: (i, 0)), pl.BlockSpec((8, 128), full)],
        out_shape=[jax.ShapeDtypeStruct((t, 128), F32), jax.ShapeDtypeStruct((8, 128), F32)],
        scratch_shapes=[pltpu.VMEM((8, 128), F32)],
        compiler_params=_cp("arbitrary"),
        name="moe_route",
    )(x32, rw, rb)


def _rowcopy_kernel(sidx_ref, didx_ref, src_ref, init_ref, dst_ref, sem, *, tb):
    del init_ref

    def copy(j):
        return pltpu.make_async_copy(src_ref.at[pl.ds(sidx_ref[0, 0, j], 1)], dst_ref.at[pl.ds(didx_ref[0, 0, j], 1)], sem)

    def start(j, c):
        copy(j).start()
        return c

    def wait(j, c):
        copy(j).wait()
        return c

    lax.fori_loop(0, tb, start, 0)
    lax.fori_loop(0, tb, wait, 0)


def _row_copy(src, sidx, didx, init, tb, name):
    n = sidx.shape[0]
    tb = min(tb, n)
    nb = n // tb
    smem = lambda: pl.BlockSpec((1, 1, tb), lambda i: (i, 0, 0), memory_space=pltpu.SMEM)
    return pl.pallas_call(
        functools.partial(_rowcopy_kernel, tb=tb),
        grid=(nb,),
        in_specs=[smem(), smem(), pl.BlockSpec(memory_space=pl.ANY), pl.BlockSpec(memory_space=pl.ANY)],
        out_specs=pl.BlockSpec(memory_space=pl.ANY),
        out_shape=jax.ShapeDtypeStruct(init.shape, init.dtype),
        scratch_shapes=[pltpu.SemaphoreType.DMA(())],
        input_output_aliases={3: 0},
        compiler_params=_cp("arbitrary"),
        name=name,
    )(sidx.reshape(nb, 1, tb), didx.reshape(nb, 1, tb), src, init)


def _experts_kernel(te_ref, nu_ref, xs_ref, wg_ref, wu_ref, wd_ref, ys_ref, acc_sc):
    del te_ref
    i = pl.program_id(0)
    j = pl.program_id(1)

    @pl.when(i < nu_ref[0])
    def _():
        @pl.when(j == 0)
        def _():
            acc_sc[...] = jnp.zeros(acc_sc.shape, F32)

        x = xs_ref[...].astype(BF16)
        act = _silu(_dot(x, wg_ref[0])) * _dot(x, wu_ref[0])
        acc_sc[...] += _dot(act.astype(BF16), wd_ref[0])

        @pl.when(j == pl.num_programs(1) - 1)
        def _():
            ys_ref[...] = acc_sc[...]

    @pl.when(jnp.logical_and(i >= nu_ref[0], j == pl.num_programs(1) - 1))
    def _():
        ys_ref[...] = jnp.zeros(ys_ref.shape, F32)


def _experts(tile_expert, n_used, xs, wgu, wd, tm, bf):
    rtot, d = xs.shape
    ne, dff, _ = wd.shape
    nf = dff // bf
    nt = rtot // tm
    row = lambda i, j, te, nu: (jnp.minimum(i, nu[0] - 1), 0)
    ex = lambda i, te, nu: te[jnp.minimum(i, nu[0] - 1)]
    return pl.pallas_call(
        _experts_kernel,
        grid_spec=pltpu.PrefetchScalarGridSpec(
            num_scalar_prefetch=2,
            grid=(nt, nf),
            in_specs=[pl.BlockSpec((tm, d), row),
                      pl.BlockSpec((1, d, bf), lambda i, j, te, nu: (ex(i, te, nu), 0, j)),
                      pl.BlockSpec((1, d, bf), lambda i, j, te, nu: (ex(i, te, nu), 0, j + nf)),
                      pl.BlockSpec((1, bf, d), lambda i, j, te, nu: (ex(i, te, nu), j, 0))],
            out_specs=pl.BlockSpec((tm, d), lambda i, j, te, nu: (i, 0)),
            scratch_shapes=[pltpu.VMEM((tm, d), F32)]),
        out_shape=jax.ShapeDtypeStruct((rtot, d), F32),
        compiler_params=_cp("arbitrary", "arbitrary"),
        name="moe_experts",
    )(tile_expert, n_used, xs, wgu, wgu, wd)


def _combine_kernel(x_ref, y1_ref, y2_ref, info_ref, g_ref, b_ref, o_ref):
    info = info_ref[...]
    w1 = info[:, RT_W1:RT_W1 + 1]
    w2 = info[:, RT_W2:RT_W2 + 1]
    o_ref[...] = _ln(ALPHA * x_ref[...] + (w1 * y1_ref[...] + w2 * y2_ref[...]), g_ref[...], b_ref[...])


def _combine(x32, yg, info, g, b, tm):
    t, d = x32.shape
    tm = min(tm, t)
    nt = t // tm
    tok = lambda i: (i, 0)
    full = lambda i: (0, 0)
    return pl.pallas_call(
        _combine_kernel,
        grid=(nt,),
        in_specs=[pl.BlockSpec((tm, d), tok), pl.BlockSpec((tm, d), tok), pl.BlockSpec((tm, d), lambda i: (i + nt, 0)),
                  pl.BlockSpec((tm, 128), tok), pl.BlockSpec((1, d), full), pl.BlockSpec((1, d), full)],
        out_specs=pl.BlockSpec((tm, d), tok),
        out_shape=jax.ShapeDtypeStruct((t, d), F32),
        compiler_params=_cp("parallel"),
        name="moe_combine_ln3",
    )(x32, yg, yg, info, g, b)


def _moe(x32, rw, rb, wgu, wd, g, b, tm, bf):
    t, d = x32.shape
    tm = min(tm, t)
    info, cnt = _route(x32, rw, rb, 512)
    i32 = jnp.int32
    e1, e2 = info[:, RT_E1].astype(i32), info[:, RT_E2].astype(i32)
    r1, r2 = info[:, RT_R1].astype(i32), info[:, RT_R2].astype(i32)
    tiles_e = (cnt[0, :N_EXPERTS].astype(i32) + tm - 1) // tm
    tile_end = jnp.cumsum(tiles_e)
    start = (tile_end - tiles_e) * tm
    nt = (2 * t) // tm + N_EXPERTS
    n_used = tile_end[N_EXPERTS - 1:]
    tile_expert = jnp.minimum(jnp.searchsorted(tile_end, jnp.arange(nt, dtype=i32), side='right'),
                              N_EXPERTS - 1).astype(i32)
    pos = jnp.concatenate([start[e1] + r1, start[e2] + r2])
    tok = jnp.arange(t, dtype=i32)
    tok2 = jnp.concatenate([tok, tok])
    xs = _row_copy(x32, tok2, pos, jnp.zeros((nt * tm, d), F32), 2048, "moe_gather")
    ys = _experts(tile_expert, n_used, xs, wgu, wd, tm, bf)
    yg = _row_copy(ys, pos, jnp.arange(2 * t, dtype=i32), jnp.zeros((2 * t, d), F32), 2048, "moe_scatter")
    return _combine(x32, yg, info, g, b, tm)


def _src_layout(with_vres):
    cols = [('m_qk', 512), ('m_v', 512), ('m_o', 512), ('m_i', 4), ('m_f', 4),
            ('f_q', 512), ('f_k', 512), ('f_v', 512), ('f_f', 8), ('gate', 3072),
            ('r_r', 512), ('r_k', 512), ('r_v', 512), ('r_w', 64), ('r_a', 64), ('r_g', 128)]
    if with_vres:
        cols.append(('r_vres', 32))
    layout, start = {}, 0
    for name, width in cols:
        layout[name] = (start, start + width)
        start += width
    return layout


def _relayout_cols(w, with_vres):
    lay = _src_layout(with_vres)
    lead = w.shape[:-1]
    z = lambda n: jnp.zeros(lead + (n,), w.dtype)
    c = lambda name: w[..., lay[name][0]:lay[name][1]]

    def slots(x, scale):
        x = x.reshape(lead + (F_HEADS, F_DH)) * scale
        return jnp.concatenate([x, jnp.zeros_like(x)], axis=-1).reshape(lead + (F_HEADS * F_PAD,))

    parts = [c('gate'), c('r_r'), c('r_k'), c('r_v'),
             c('r_w'), c('r_a'), c('r_g'),
             c('m_i'), c('m_f'), c('f_f'), z(16), c('r_vres') if with_vres else z(32), z(64), z(128),
             slots(c('f_q'), F_DH ** -0.5 * LOG2E), slots(c('f_k'), 1.0),
             c('m_qk'), c('m_v'), c('m_o'), c('f_v')]
    return jnp.concatenate(parts, axis=-1)


def _row(v):
    return v.reshape(1, -1).astype(F32)


def _branches(x16, bsz, s, prm, v_first):
    (w_in, b_in, m_conv, m_norm, m_up, f_up, r_mu, r_wbias, r_wB, r_abias, r_aB, r_vbias, r_vB,
     r_gB, r_kk, r_ka, r_rk, r_ln_g, r_ln_b, r_up, w_out, ln1_g, ln1_b) = prm
    with_vres = r_vbias is not None
    w_all = _relayout_cols(w_in, with_vres)
    b_all = _relayout_cols(b_in, with_vres).reshape(1, NP)
    p16 = _matmul_bias(x16, w_all.astype(BF16), b_all, BF16, 1024, 1024, "in_proj")
    gates = _matmul_bias(x16, w_all[:, C_SMALL:C_SMALL + 128].astype(BF16), b_all[:, C_SMALL:C_SMALL + 128],
                         F32, 2048, 128, "in_proj_gates")
    p16 = p16.reshape(bsz, s, NP)
    gates = gates.reshape(bsz, s, 128)
    gates_t = jnp.transpose(gates, (0, 2, 1))

    h_a = _mlstm(p16, gates, gates_t, m_conv.astype(F32), _row(m_norm), 256)
    q_aug, k_aug, v_aug = _fox_prep(p16, gates_t, 512)
    h_b = _fox_attention(q_aug, k_aug, v_aug, 512)
    lay = _src_layout(with_vres)
    r0 = lay['r_r'][0]
    mu = r_mu[:1536].reshape(1, 1536)
    mum = jnp.concatenate([r_mu[1536:1792], jnp.zeros((32,), F32),
                           r_mu[lay['r_vres'][0] - r0:] if with_vres else jnp.zeros((32,), F32),
                           jnp.zeros((192,), F32)]).reshape(1, 512)
    z = jnp.zeros((64, 512), F32)
    w_wa = jnp.concatenate([jnp.concatenate([r_wB, z], axis=1), jnp.concatenate([z, r_aB], axis=1)], axis=0).astype(BF16)
    w_v = None
    if with_vres:
        w_v = jnp.concatenate([jnp.zeros((32, 512), F32), r_vB, jnp.zeros((64, 512), F32)], axis=0).astype(BF16)
    hid = jnp.arange(512) // R_DH
    headsum = (hid[:, None] == hid[None, :]).astype(BF16)
    outs = _rwkv_prep(p16, v_first, mu, mum, w_wa, w_v, r_gB.astype(BF16), headsum, _row(r_wbias), _row(r_abias),
                      _row(r_vbias) if with_vres else None, _row(r_kk), _row(r_ka), _row(r_rk), 256)
    r, lw, k2, v, kk, b, g, bv = outs[:8]
    v_own = v_first if with_vres else outs[8]
    m_c, n_c, rq, y0 = _rwkv_chunks(r, lw, k2, v, kk, b, 2)
    y = _rwkv_scan(m_c, n_c, rq, y0, bv, g, _row(r_ln_g), _row(r_ln_b), 8, 2)
    return p16, h_a, h_b, y, v_own


def _token_mixer(x16, x32, bsz, s, prm, v_first):
    m_up, f_up, r_up, w_out, ln1_g, ln1_b = prm[4], prm[5], prm[19], prm[20], prm[21], prm[22]
    t = bsz * s
    p16, h_a, h_b, y, v_own = _branches(x16, bsz, s, prm, v_first)
    x1_32, x1_16 = _merge(h_a.reshape(t, 512), h_b.reshape(t, 512), y.reshape(t, 512), p16.reshape(t, NP), x32,
                          m_up.astype(BF16), f_up.astype(BF16), r_up.astype(BF16), w_out.astype(BF16),
                          _row(ln1_g), _row(ln1_b), 512)
    return x1_32, x1_16, v_own


def kernel(x, mem, mem_ln_g, mem_ln_b, w_in_0, b_in_0, m_conv_0, m_norm_0, m_up_0, f_up_0, r_mu_0, r_wbias_0, r_wB_0, r_abias_0, r_aB_0, r_gB_0, r_kk_0, r_ka_0, r_rk_0, r_ln_g_0, r_ln_b_0, r_up_0, w_out_0, ln1_g_0, ln1_b_0, x_wq_0, x_wkv_0, x_wo_0, ln2_g_0, ln2_b_0, ff_wgu_0, ff_wd_0, ln3_g_0, ln3_b_0, w_in_1, b_in_1, m_conv_1, m_norm_1, m_up_1, f_up_1, r_mu_1, r_wbias_1, r_wB_1, r_abias_1, r_aB_1, r_vbias_1, r_vB_1, r_gB_1, r_kk_1, r_ka_1, r_rk_1, r_ln_g_1, r_ln_b_1, r_up_1, w_out_1, ln1_g_1, ln1_b_1, x_wq_1, x_wkv_1, x_wo_1, ln2_g_1, ln2_b_1, ex_router_1, ex_router_b_1, ex_wgu_1, ex_wd_1, ln3_g_1, ln3_b_1):
    bsz, s, d = x.shape
    t = bsz * s
    x32 = x.reshape(t, d)
    x16 = x32.astype(BF16)

    mixers = (
        (w_in_0, b_in_0, m_conv_0, m_norm_0, m_up_0, f_up_0, r_mu_0, r_wbias_0, r_wB_0, r_abias_0, r_aB_0, None, None,
         r_gB_0, r_kk_0, r_ka_0, r_rk_0, r_ln_g_0, r_ln_b_0, r_up_0, w_out_0, ln1_g_0, ln1_b_0),
        (w_in_1, b_in_1, m_conv_1, m_norm_1, m_up_1, f_up_1, r_mu_1, r_wbias_1, r_wB_1, r_abias_1, r_aB_1, r_vbias_1, r_vB_1,
         r_gB_1, r_kk_1, r_ka_1, r_rk_1, r_ln_g_1, r_ln_b_1, r_up_1, w_out_1, ln1_g_1, ln1_b_1),
    )
    xattn = ((x_wq_0, x_wo_0, ln2_g_0, ln2_b_0), (x_wq_1, x_wo_1, ln2_g_1, ln2_b_1))
    kv0, kv1 = _mem_kv(mem, _row(mem_ln_g), _row(mem_ln_b), x_wkv_0.astype(BF16), x_wkv_1.astype(BF16))
    kvs = (kv0, kv1)

    v_first = None
    for l in range(DEPTH):
        x32, x16, v_own = _token_mixer(x16, x32, bsz, s, mixers[l], v_first)
        if l == 0:
            v_first = v_own
        wq, wo, g2, b2 = xattn[l]
        kv = kvs[l]
        x32, x16 = _xattn(x16.reshape(bsz, s, d), x32.reshape(bsz, s, d), kv[:, :, :d], kv[:, :, d:],
                          wq.astype(BF16), wo.astype(BF16), _row(g2), _row(b2), 512)
        x32, x16 = x32.reshape(t, d), x16.reshape(t, d)
        if l % 2 == 0:
            x32, x16 = _ffn(x16, x32, ff_wgu_0.astype(BF16), ff_wd_0.astype(BF16), _row(ln3_g_0), _row(ln3_b_0), 512, 1408)
        else:
            rw = jnp.concatenate([ex_router_1, jnp.zeros((d, 128 - N_EXPERTS), F32)], axis=1)
            rb = jnp.concatenate([ex_router_b_1, jnp.zeros((128 - N_EXPERTS,), F32)]).reshape(1, 128)
            x32 = _moe(x32, rw, rb, ex_wgu_1.astype(BF16), ex_wd_1.astype(BF16), _row(ln3_g_1), _row(ln3_b_1), 1024, 896)
    return x32.reshape(bsz, s, d)
```

```python
import functools

import jax
import jax.numpy as jnp
from jax import lax
from jax.experimental import pallas as pl
from jax.experimental.pallas import tpu as pltpu

F32 = jnp.float32
BF16 = jnp.bfloat16

D_MODEL = 1024
DEPTH = 2
M_HEADS, M_DV, M_DQK, M_CONV, M_CHUNK = 4, 128, 64, 4, 64
F_HEADS, F_DH, F_PAD = 8, 64, 128
R_HEADS, R_DH, R_CHUNK = 8, 64, 64
R_GN_EPS = 64e-5
X_HEADS, X_DH = 4, 256
N_EXPERTS = 8
ALPHA = (2.0 * DEPTH) ** 0.25
LN_EPS = 1e-5
HEAD_NORM_EPS = 1e-6
NEG = -1e30
LOG2E = 1.4426950408889634

C_GATE = 0
C_RKV = 3072
C_RMISC = 4608
C_SMALL = 4864
C_FQ = 5120
C_FK = 6144
C_MQK = 7168
C_MV = 7680
C_MO = 8192
C_FV = 8704
NP = 9216
VMEM_LIMIT = 56 * 1024 * 1024


def _cp(*sem):
    return pltpu.CompilerParams(dimension_semantics=sem, vmem_limit_bytes=VMEM_LIMIT)


def _dot(a, b):
    return jnp.dot(a, b, preferred_element_type=F32)


def _dot_nt(a, b):
    return lax.dot_general(a, b, (((1,), (1,)), ((), ())), preferred_element_type=F32)


def _dot_tn(a, b):
    return lax.dot_general(a, b, (((0,), (0,)), ((), ())), preferred_element_type=F32)


def _split(x):
    hi = x.astype(BF16)
    lo = (x - hi.astype(F32)).astype(BF16)
    return hi, lo


def _split3(x):
    h1 = x.astype(BF16)
    r1 = x - h1.astype(F32)
    h2 = r1.astype(BF16)
    h3 = (r1 - h2.astype(F32)).astype(BF16)
    return h1, h2, h3


def _dot_x2(a, b16):
    hi, lo = _split(a)
    return _dot(hi, b16) + _dot(lo, b16)


def _dot_x3(a, b):
    ah, al = _split(a)
    bh, bl = _split(b)
    return _dot(ah, bh) + (_dot(al, bh) + _dot(ah, bl))


def _dot16(a, b):
    return _dot(a.astype(BF16), b.astype(BF16))


def _sigmoid(x):
    return 1.0 / (1.0 + jnp.exp(-x))


def _log_sigmoid(x):
    return jnp.minimum(x, 0.0) - jnp.log(1.0 + jnp.exp(-jnp.abs(x)))


def _softplus(x):
    return jnp.maximum(x, 0.0) + jnp.log(1.0 + jnp.exp(-jnp.abs(x)))


def _silu(x):
    return x * _sigmoid(x)


def _ln(z, g, b, eps=LN_EPS):
    mu = jnp.mean(z, axis=-1, keepdims=True)
    zc = z - mu
    var = jnp.mean(zc * zc, axis=-1, keepdims=True)
    return zc * lax.rsqrt(var + eps) * g + b


def _mm_kernel(a_ref, w_ref, b_ref, o_ref):
    o_ref[...] = (_dot(a_ref[...], w_ref[...]) + b_ref[...]).astype(o_ref.dtype)


def _matmul_bias(a, w, b, out_dtype, tm, tn, name):
    m, k = a.shape
    n = w.shape[1]
    tm, tn = min(tm, m), min(tn, n)
    return pl.pallas_call(
        _mm_kernel,
        grid=(m // tm, n // tn),
        in_specs=[pl.BlockSpec((tm, k), lambda i, j: (i, 0)),
                  pl.BlockSpec((k, tn), lambda i, j: (0, j)),
                  pl.BlockSpec((1, tn), lambda i, j: (0, j))],
        out_specs=pl.BlockSpec((tm, tn), lambda i, j: (i, j)),
        out_shape=jax.ShapeDtypeStruct((m, n), out_dtype),
        compiler_params=_cp("parallel", "parallel"),
        name=name,
    )(a, w, b)


def _memkv_kernel(mem_ref, g_ref, b_ref, w0_ref, w1_ref, o0_ref, o1_ref):
    mn = _ln(mem_ref[0], g_ref[...], b_ref[...]).astype(BF16)
    o0_ref[0] = _dot(mn, w0_ref[...]).astype(BF16)
    o1_ref[0] = _dot(mn, w1_ref[...]).astype(BF16)


def _mem_kv(mem, g, b, wkv0, wkv1):
    bsz, m, d = mem.shape
    n = wkv0.shape[1]
    full = lambda i: (0, 0)
    return pl.pallas_call(
        _memkv_kernel,
        grid=(bsz,),
        in_specs=[pl.BlockSpec((1, m, d), lambda i: (i, 0, 0)),
                  pl.BlockSpec((1, d), full), pl.BlockSpec((1, d), full),
                  pl.BlockSpec((d, n), full), pl.BlockSpec((d, n), full)],
        out_specs=[pl.BlockSpec((1, m, n), lambda i: (i, 0, 0))] * 2,
        out_shape=[jax.ShapeDtypeStruct((bsz, m, n), BF16)] * 2,
        compiler_params=_cp("parallel"),
        name="mem_kv",
    )(mem, g, b, wkv0, wkv1)


def _foxprep_kernel(g_ref, q_ref, k_ref, v_ref, qo_ref, ko_ref, vo_ref, carry_sc, *, tt):
    blk = 128

    @pl.when(pl.program_id(1) == 0)
    def _():
        carry_sc[...] = jnp.zeros(carry_sc.shape, F32)

    r = lax.broadcasted_iota(jnp.int32, (blk, blk), 0)
    c = lax.broadcasted_iota(jnp.int32, (blk, blk), 1)
    upper = (r <= c).astype(BF16)
    carry = carry_sc[:, 0:1]
    parts = []
    for i in range(tt // blk):
        ls = _log_sigmoid(g_ref[0, :, i * blk:(i + 1) * blk])
        h1, h2, h3 = _split3(ls)
        cs = _dot(h1, upper) + (_dot(h2, upper) + _dot(h3, upper)) + carry
        carry = cs[:, blk - 1:blk]
        parts.append(cs)
    carry_sc[...] = jnp.broadcast_to(carry, carry_sc.shape)
    cum = jnp.concatenate(parts, axis=1) * LOG2E
    c1, c2, c3 = [p.astype(F32) for p in _split3(cum)]
    w = F_HEADS * F_PAD
    row = lax.broadcasted_iota(jnp.int32, (F_HEADS, w), 0)
    lane = lax.broadcasted_iota(jnp.int32, (F_HEADS, w), 1)
    sel = lambda j: (lane == row * F_PAD + F_DH + j).astype(F32)
    lane1 = lax.broadcasted_iota(jnp.int32, (1, w), 1) % F_PAD
    ones_q = jnp.logical_and(lane1 >= F_DH + 3, lane1 < F_DH + 6).astype(F32)
    ones_k = jnp.logical_and(lane1 >= F_DH, lane1 < F_DH + 3).astype(F32)
    aug_q = _dot_tn(c1, sel(0)) + _dot_tn(c2, sel(1)) + _dot_tn(c3, sel(2)) + ones_q
    aug_k = ones_k - (_dot_tn(c1, sel(3)) + _dot_tn(c2, sel(4)) + _dot_tn(c3, sel(5)))
    qo_ref[0] = (q_ref[0].astype(F32) + aug_q).astype(BF16)
    ko_ref[0] = (k_ref[0].astype(F32) + aug_k).astype(BF16)
    wv = F_HEADS * F_DH
    src = lax.broadcasted_iota(jnp.int32, (wv, w), 0)
    dst = lax.broadcasted_iota(jnp.int32, (wv, w), 1)
    place = (dst == (src // F_DH) * F_PAD + src % F_DH).astype(BF16)
    ones_v = (lane1 >= F_DH).astype(F32)
    vo_ref[0] = jnp.transpose(_dot(v_ref[0], place) + ones_v).astype(BF16)


def _fox_prep(p16, gates_t, tt):
    bsz, s, _ = p16.shape
    tt = min(tt, s)
    w = F_HEADS * F_PAD
    wv = F_HEADS * F_DH
    tok = lambda b, i: (b, i, 0)
    return pl.pallas_call(
        functools.partial(_foxprep_kernel, tt=tt),
        grid=(bsz, s // tt),
        in_specs=[pl.BlockSpec((1, 8, tt), lambda b, i: (b, 1, i)),
                  pl.BlockSpec((1, tt, w), lambda b, i: (b, i, C_FQ // w)),
                  pl.BlockSpec((1, tt, w), lambda b, i: (b, i, C_FK // w)),
                  pl.BlockSpec((1, tt, wv), lambda b, i: (b, i, C_FV // wv))],
        out_specs=[pl.BlockSpec((1, tt, w), tok), pl.BlockSpec((1, tt, w), tok),
                   pl.BlockSpec((1, w, tt), lambda b, i: (b, 0, i))],
        out_shape=[jax.ShapeDtypeStruct((bsz, s, w), BF16), jax.ShapeDtypeStruct((bsz, s, w), BF16),
                   jax.ShapeDtypeStruct((bsz, w, s), BF16)],
        scratch_shapes=[pltpu.VMEM((8, 128), F32)],
        compiler_params=_cp("parallel", "arbitrary"),
        name="fox_prep",
    )(gates_t, p16, p16, p16)


def _fox_kernel(qt_ref, kt_ref, q_ref, k_ref, v_ref, o_ref, m_sc, acc_sc, *, t):
    qi = qt_ref[pl.program_id(1)]
    ki = kt_ref[pl.program_id(1)]

    @pl.when(ki == 0)
    def _():
        m_sc[...] = jnp.full(m_sc.shape, NEG, F32)
        acc_sc[...] = jnp.zeros(acc_sc.shape, F32)

    def block(diag):
        if diag:
            key = lax.broadcasted_iota(jnp.int32, (t, t), 0)
            qry = lax.broadcasted_iota(jnp.int32, (t, t), 1)
            mask = key <= qry
        pss = [slice(h * F_PAD, (h + 1) * F_PAD) for h in range(F_HEADS)]
        scores = lambda h: _dot_nt(k_ref[0, :, pss[h]], q_ref[0, :, pss[h]])
        s_next = scores(0)
        for h in range(F_HEADS):
            s = s_next
            if h + 1 < F_HEADS:
                s_next = scores(h + 1)
            if diag:
                s = jnp.where(mask, s, NEG)
            m_old = m_sc[h]
            m_new = jnp.maximum(m_old, jnp.max(s, axis=0, keepdims=True))
            p = jnp.exp2(s - m_new).astype(BF16)
            acc_sc[h] = jnp.exp2(m_old - m_new) * acc_sc[h] + _dot(v_ref[0, pss[h], :], p)
            m_sc[h] = m_new

    @pl.when(ki < qi)
    def _():
        block(False)

    @pl.when(ki == qi)
    def _():
        block(True)
        for h in range(F_HEADS):
            acc = acc_sc[h]
            out_t = acc[:F_DH, :] / acc[F_DH:F_DH + 1, :]
            o_ref[0, :, h * F_DH:(h + 1) * F_DH] = jnp.transpose(out_t).astype(o_ref.dtype)


def _fox_attention(q_aug, k_aug, v_aug_t, t):
    bsz, s, w = q_aug.shape
    t = min(t, s)
    n = s // t
    wv = F_HEADS * F_DH
    pairs = [(i, j) for i in range(n) for j in range(i + 1)]
    qt = jnp.asarray([p[0] for p in pairs], jnp.int32)
    kt = jnp.asarray([p[1] for p in pairs], jnp.int32)
    return pl.pallas_call(
        functools.partial(_fox_kernel, t=t),
        grid_spec=pltpu.PrefetchScalarGridSpec(
            num_scalar_prefetch=2,
            grid=(bsz, len(pairs)),
            in_specs=[pl.BlockSpec((1, t, w), lambda b, p, qt, kt: (b, qt[p], 0)),
                      pl.BlockSpec((1, t, w), lambda b, p, qt, kt: (b, kt[p], 0)),
                      pl.BlockSpec((1, w, t), lambda b, p, qt, kt: (b, 0, kt[p]))],
            out_specs=pl.BlockSpec((1, t, wv), lambda b, p, qt, kt: (b, qt[p], 0)),
            scratch_shapes=[pltpu.VMEM((F_HEADS, 1, t), F32), pltpu.VMEM((F_HEADS, F_PAD, t), F32)]),
        out_shape=jax.ShapeDtypeStruct((bsz, s, wv), BF16),
        compiler_params=_cp("parallel", "arbitrary"),
        name="fox_attention",
    )(qt, kt, q_aug, k_aug, v_aug_t)


def _mlstm_kernel(qk_ref, v_ref, o_ref, g_ref, gt_ref, cw_ref, nw_ref, out_ref,
                  prev_sc, q_sc, k_sc, c_sc, m_sc, *, lt):
    ti = pl.program_id(1)
    L = M_CHUNK
    H = range(M_HEADS)

    @pl.when(ti == 0)
    def _():
        prev_sc[...] = jnp.zeros(prev_sc.shape, F32)
        c_sc[...] = jnp.zeros(c_sc.shape, F32)
        m_sc[...] = jnp.zeros(m_sc.shape, F32)

    u = qk_ref[0].astype(F32)
    prev = prev_sc[...]
    rows = lax.broadcasted_iota(jnp.int32, u.shape, 0)
    acc = u * cw_ref[M_CONV - 1:M_CONV, :]
    for d in range(1, M_CONV):
        sh = jnp.where(rows < d, pltpu.roll(prev, d, 0), pltpu.roll(u, d, 0))
        acc = acc + sh * cw_ref[M_CONV - 1 - d:M_CONV - d, :]
    prev_sc[...] = u
    qk = _silu(acc)
    nq = M_HEADS * M_DQK
    q_sc[...] = qk[:, :nq].astype(BF16)
    k_sc[...] = qk[:, nq:] * (M_DQK ** -0.5)

    r = lax.broadcasted_iota(jnp.int32, (L, L), 0)
    c = lax.broadcasted_iota(jnp.int32, (L, L), 1)
    tri = c <= r
    lower = tri.astype(BF16)
    upper = (r <= c).astype(BF16)
    ones = jnp.ones((L, M_DV), BF16)

    for ch in range(lt // L):
        rs = slice(ch * L, (ch + 1) * L)
        gc = g_ref[0, rs, :]
        gr = gt_ref[0, :, rs]
        lc1, lc2 = _split(_log_sigmoid(gc))
        lr1, lr2 = _split(_log_sigmoid(gr))
        bcol = _dot(lower, lc1) + _dot(lower, lc2)
        brow = _dot(lr1, upper) + _dot(lr2, upper)
        hsl = [slice(h * M_DV, (h + 1) * M_DV) for h in H]
        qsl = [slice(h * M_DQK, (h + 1) * M_DQK) for h in H]
        bc = [bcol[:, 4 + h:5 + h] for h in H]
        m_prev = [m_sc[h:h + 1, 0:1] for h in H]
        qh = [q_sc[rs, qsl[h]] for h in H]
        kh = [k_sc[rs, qsl[h]] for h in H]
        vext = [jnp.concatenate([v_ref[0, rs, hsl[h]], ones], axis=1) for h in H]
        cext = [c_sc[h] for h in H]
        dm = [jnp.where(tri, bc[h] - brow[4 + h:5 + h, :] + gr[h:h + 1, :], -jnp.inf) for h in H]
        d_inter = [bc[h] + m_prev[h] for h in H]
        m_t = [jnp.maximum(d_inter[h], jnp.max(dm[h], axis=1, keepdims=True)) for h in H]
        qk_s = [_dot_nt(qh[h], kh[h].astype(BF16)) for h in H]
        qc = [_dot(qh[h], cext[h].astype(BF16)) for h in H]
        w_intra = [(qk_s[h] * jnp.exp(dm[h] - m_t[h])).astype(BF16) for h in H]
        ext = [jnp.exp(d_inter[h] - m_t[h]) * qc[h] + _dot(w_intra[h], vext[h]) for h in H]
        b_last = [bc[h][L - 1:L, :] for h in H]
        g = [b_last[h] - bc[h] + gc[:, h:h + 1] for h in H]
        m_new = [jnp.maximum(b_last[h] + m_prev[h], jnp.max(g[h], axis=0, keepdims=True)) for h in H]
        wk = [(jnp.exp(g[h] - m_new[h]) * kh[h]).astype(BF16) for h in H]
        upd = [_dot_tn(wk[h], vext[h]) for h in H]
        for h in H:
            c_sc[h] = jnp.exp(b_last[h] + m_prev[h] - m_new[h]) * cext[h] + upd[h]
            m_sc[h:h + 1, :] = jnp.broadcast_to(m_new[h], (1, m_sc.shape[1]))
        for h in H:
            hh = ext[h][:, :M_DV] / jnp.maximum(jnp.abs(ext[h][:, M_DV:M_DV + 1]), jnp.exp(-m_t[h]))
            mu = jnp.mean(hh, axis=1, keepdims=True)
            hc = hh - mu
            var = jnp.mean(hc * hc, axis=1, keepdims=True)
            hn = hc * lax.rsqrt(var + HEAD_NORM_EPS) * nw_ref[:, hsl[h]]
            out_ref[0, rs, hsl[h]] = (hn * _sigmoid(o_ref[0, rs, hsl[h]].astype(F32))).astype(out_ref.dtype)


def _mlstm(p16, gates, gates_t, conv_w, norm_w, lt):
    bsz, s, _ = p16.shape
    lt = min(lt, s)
    w = M_HEADS * M_DV
    full = lambda b, i: (0, 0)
    return pl.pallas_call(
        functools.partial(_mlstm_kernel, lt=lt),
        grid=(bsz, s // lt),
        in_specs=[pl.BlockSpec((1, lt, w), lambda b, i: (b, i, C_MQK // w)),
                  pl.BlockSpec((1, lt, w), lambda b, i: (b, i, C_MV // w)),
                  pl.BlockSpec((1, lt, w), lambda b, i: (b, i, C_MO // w)),
                  pl.BlockSpec((1, lt, 128), lambda b, i: (b, i, 0)),
                  pl.BlockSpec((1, 8, lt), lambda b, i: (b, 0, i)),
                  pl.BlockSpec((M_CONV, w), full),
                  pl.BlockSpec((1, w), full)],
        out_specs=pl.BlockSpec((1, lt, w), lambda b, i: (b, i, 0)),
        out_shape=jax.ShapeDtypeStruct((bsz, s, w), BF16),
        scratch_shapes=[pltpu.VMEM((lt, w), F32),
                        pltpu.VMEM((lt, M_HEADS * M_DQK), BF16), pltpu.VMEM((lt, M_HEADS * M_DQK), F32),
                        pltpu.VMEM((M_HEADS, M_DQK, 2 * M_DV), F32), pltpu.VMEM((8, 128), F32)],
        compiler_params=_cp("parallel", "arbitrary"),
        name="mlstm",
    )(p16, p16, p16, gates, gates_t, conv_w, norm_w)


def _rprep_kernel(*refs, tt, has_vres):
    if has_vres:
        (p_ref, pm_ref, vf_ref, mu_ref, mum_ref, wa_ref, wv_ref, wg_ref, hs_ref, wbias_ref, abias_ref, vbias_ref,
         kkw_ref, kaw_ref, rkw_ref,
         r_o, lw_o, k_o, v_o, kk_o, b_o, g_o, bv_o, carry_sc, carrym_sc) = refs
    else:
        (p_ref, pm_ref, mu_ref, mum_ref, wa_ref, wg_ref, hs_ref, wbias_ref, abias_ref,
         kkw_ref, kaw_ref, rkw_ref,
         r_o, lw_o, k_o, v_o, kk_o, b_o, g_o, bv_o, vown_o, carry_sc, carrym_sc) = refs
    ti = pl.program_id(1)

    @pl.when(ti == 0)
    def _():
        carry_sc[...] = jnp.zeros(carry_sc.shape, F32)
        carrym_sc[...] = jnp.zeros(carrym_sc.shape, F32)

    rows = lax.broadcasted_iota(jnp.int32, (tt, 512), 0)

    def shifted(ref, cref, mref, c0, width):
        x = ref[0, :, c0:c0 + width].astype(F32)
        prev = jnp.where(rows[:, :width] == 0, cref[0:1, c0:c0 + width], pltpu.roll(x, 1, 0))
        return x + mref[:, c0:c0 + width] * (prev - x)

    r = shifted(p_ref, carry_sc, mu_ref, 0, 512)
    k = shifted(p_ref, carry_sc, mu_ref, 512, 512)
    v = shifted(p_ref, carry_sc, mu_ref, 1024, 512)
    wa_in = shifted(pm_ref, carrym_sc, mum_ref, 0, 128)
    g_in = shifted(pm_ref, carrym_sc, mum_ref, 128, 128)
    lane = lax.broadcasted_iota(jnp.int32, (tt, 128), 1)
    wa_act = jnp.where(lane < 64, jnp.tanh(wa_in), wa_in).astype(BF16)
    wa = _dot(wa_act, wa_ref[...])
    w_log = -_softplus(-(wbias_ref[...] + wa[:, :512])) - 0.5
    lw_o[0] = -jnp.exp(w_log)
    a = _sigmoid(abias_ref[...] + wa[:, 512:])
    if has_vres:
        s_in = shifted(pm_ref, carrym_sc, mum_ref, 256, 128)
        mix = _sigmoid(vbias_ref[...] + _dot(s_in.astype(BF16), wv_ref[...]))
        v = v + (vf_ref[0].astype(F32) - v) * mix
    else:
        vown_o[0] = v.astype(vown_o.dtype)
    g_o[0] = _dot(_sigmoid(g_in).astype(BF16), wg_ref[...]).astype(g_o.dtype)
    hs = hs_ref[...]
    u = k * kkw_ref[...]
    nrm = jnp.sqrt(_dot_x2(u * u, hs))
    kk = u / jnp.maximum(nrm, 1e-12)
    k2 = k * (1.0 + (a - 1.0) * kaw_ref[...])
    bonus = _dot_x2(r * k2 * rkw_ref[...], hs)
    r_o[0] = r.astype(r_o.dtype)
    k_o[0] = k2.astype(k_o.dtype)
    v_o[0] = v.astype(v_o.dtype)
    kk_o[0] = kk.astype(kk_o.dtype)
    b_o[0] = (kk * a).astype(b_o.dtype)
    bv_o[0] = (bonus * v).astype(bv_o.dtype)
    carry_sc[0:1, :] = p_ref[0, tt - 1:tt, :].astype(F32)
    carrym_sc[0:1, :] = pm_ref[0, tt - 1:tt, :].astype(F32)


def _rwkv_prep(p16, v_first, mu, mum, w_wa, w_v, w_g, headsum, wbias, abias, vbias, kkw, kaw, rkw, tt):
    bsz, s, _ = p16.shape
    tt = min(tt, s)
    has_vres = v_first is not None
    full = lambda b, i: (0, 0)
    tok = lambda b, i: (b, i, 0)
    vec = pl.BlockSpec((1, 512), full)
    ins = [p16, p16]
    specs = [pl.BlockSpec((1, tt, 1536), lambda b, i: (b, i, C_RKV // 1536)),
             pl.BlockSpec((1, tt, 512), lambda b, i: (b, i, C_RMISC // 512))]
    if has_vres:
        ins.append(v_first)
        specs.append(pl.BlockSpec((1, tt, 512), tok))
    ins += [mu, mum, w_wa]
    specs += [pl.BlockSpec((1, 1536), full), pl.BlockSpec((1, 512), full), pl.BlockSpec((128, 1024), full)]
    if has_vres:
        ins.append(w_v)
        specs.append(pl.BlockSpec((128, 512), full))
    ins += [w_g, headsum, wbias, abias]
    specs += [pl.BlockSpec((128, 512), full), pl.BlockSpec((512, 512), full), vec, vec]
    if has_vres:
        ins.append(vbias)
        specs.append(vec)
    ins += [kkw, kaw, rkw]
    specs += [vec, vec, vec]
    n_out = 8 if has_vres else 9
    dts = [BF16, F32, BF16, BF16, BF16, BF16, BF16, BF16] + ([] if has_vres else [BF16])
    return pl.pallas_call(
        functools.partial(_rprep_kernel, tt=tt, has_vres=has_vres),
        grid=(bsz, s // tt),
        in_specs=specs,
        out_specs=[pl.BlockSpec((1, tt, 512), tok)] * n_out,
        out_shape=[jax.ShapeDtypeStruct((bsz, s, 512), dt) for dt in dts],
        scratch_shapes=[pltpu.VMEM((8, 1536), F32), pltpu.VMEM((8, 512), F32)],
        compiler_params=_cp("parallel", "arbitrary"),
        name="rwkv_prep",
    )(*ins)


def _rchunk_kernel(r_ref, lw_ref, k_ref, v_ref, kk_ref, b_ref, m_o, n_o, rq_o, y0_o, *, nck):
    L = R_CHUNK
    ri = lax.broadcasted_iota(jnp.int32, (L, L), 0)
    ci = lax.broadcasted_iota(jnp.int32, (L, L), 1)
    lower = (ci <= ri).astype(BF16)
    strict = ci < ri
    incl = ci <= ri
    eye = (ci == ri).astype(F32)

    kk_t, r_t, b_h, k_h, b_w, k_w, w_last, vf = [], [], [], [], [], [], [], []
    for c in range(nck):
        rs = slice(c * L, (c + 1) * L)
        lw = lw_ref[0, rs, :]
        l1, l2 = _split(lw)
        cl = _dot(lower, l1) + _dot(lower, l2)
        cl_last = cl[L - 1:L, :]
        w_inv = jnp.exp(-cl)
        w_rem = jnp.exp(cl_last - cl)
        kf = k_ref[0, rs, :].astype(F32)
        bf = b_ref[0, rs, :].astype(F32)
        kk_t.append((kk_ref[0, rs, :].astype(F32) * jnp.exp(cl - lw)).astype(BF16))
        r_t.append(r_ref[0, rs, :].astype(F32) * jnp.exp(cl))
        b_h.append((bf * w_inv).astype(BF16))
        k_h.append((kf * w_inv).astype(BF16))
        b_w.append((bf * w_rem).astype(BF16))
        k_w.append((kf * w_rem).astype(BF16))
        w_last.append(jnp.exp(cl_last))
        vf.append(v_ref[0, rs, :])

    J = [(c, slice(h * R_DH, (h + 1) * R_DH)) for c in range(nck) for h in range(R_HEADS)]
    N = range(len(J))
    a = [_dot_nt(jnp.concatenate([kk_t[c][:, sl], r_t[c][:, sl].astype(BF16)], axis=0),
                 jnp.concatenate([b_h[c][:, sl], k_h[c][:, sl]], axis=0)) for c, sl in J]
    a_ak = [jnp.where(strict, a[j][:L, L:], 0.0).astype(BF16) for j in N]
    a_rb = [jnp.where(incl, a[j][L:, :L], 0.0).astype(BF16) for j in N]
    a_rk = [jnp.where(incl, a[j][L:, L:], 0.0).astype(BF16) for j in N]
    x = [jnp.where(strict, -a[j][:L, :L], 0.0) for j in N]
    t = [eye + x[j] for j in N]
    for _ in range(5):
        x = [_dot16(x[j], x[j]) for j in N]
        t = [t[j] + _dot16(t[j], x[j]) for j in N]
    vh = [vf[c][:, sl] for c, sl in J]
    akv = [_dot(a_ak[j], vh[j]) for j in N]
    rkv = [_dot(a_rk[j], vh[j]) for j in N]
    pq16 = [_dot16(t[j], jnp.concatenate([kk_t[c][:, sl].astype(F32), akv[j]], axis=1)).astype(BF16)
            for j, (c, sl) in enumerate(J)]
    rq_y0 = [jnp.concatenate([r_t[c][:, sl], rkv[j]], axis=1) - _dot(a_rb[j], pq16[j]) for j, (c, sl) in enumerate(J)]
    pqb = [_dot_tn(pq16[j], b_w[c][:, sl]) for j, (c, sl) in enumerate(J)]
    vk = [_dot_tn(vh[j], k_w[c][:, sl]) for j, (c, sl) in enumerate(J)]
    for j, (c, sl) in enumerate(J):
        rs = slice(c * L, (c + 1) * L)
        rq_o[0, rs, sl] = rq_y0[j][:, :R_DH]
        y0_o[0, rs, sl] = rq_y0[j][:, R_DH:]
        h = j % R_HEADS
        m_o[0, c, h] = eye * w_last[c][:, sl] - pqb[j][:R_DH]
        n_o[0, c, h] = vk[j] - pqb[j][R_DH:]


def _rwkv_chunks(r, lw, k, v, kk, b, nck):
    bsz, s, w = r.shape
    L = R_CHUNK
    nc = s // L
    tok = lambda bi, c: (bi, c, 0)
    st = lambda bi, c: (bi, c, 0, 0, 0)
    sshape = (bsz, nc, R_HEADS, R_DH, R_DH)
    sblock = (1, nck, R_HEADS, R_DH, R_DH)
    return pl.pallas_call(
        functools.partial(_rchunk_kernel, nck=nck),
        grid=(bsz, nc // nck),
        in_specs=[pl.BlockSpec((1, nck * L, w), tok)] * 6,
        out_specs=[pl.BlockSpec(sblock, st), pl.BlockSpec(sblock, st),
                   pl.BlockSpec((1, nck * L, w), tok), pl.BlockSpec((1, nck * L, w), tok)],
        out_shape=[jax.ShapeDtypeStruct(sshape, F32), jax.ShapeDtypeStruct(sshape, F32),
                   jax.ShapeDtypeStruct((bsz, s, w), F32), jax.ShapeDtypeStruct((bsz, s, w), F32)],
        compiler_params=_cp("parallel", "parallel"),
        name="rwkv_chunks",
    )(r, lw, k, v, kk, b)


def _rstate_kernel(m_ref, n_ref, s_ref, s_sc, *, nck):
    H = range(R_HEADS)

    @pl.when(pl.program_id(1) == 0)
    def _():
        s_sc[...] = jnp.zeros(s_sc.shape, F32)

    for c in range(nck):
        st = [s_sc[h] for h in H]
        sp = [_split(st[h]) for h in H]
        mp = [_split(m_ref[0, c, h]) for h in H]
        snew = [_dot(sp[h][0], mp[h][0]) + (_dot(sp[h][1], mp[h][0]) + _dot(sp[h][0], mp[h][1])) for h in H]
        for h in H:
            s_ref[0, c, h] = st[h]
            s_sc[h] = snew[h] + n_ref[0, c, h]


def _rout_kernel(s_ref, rq_ref, y0_ref, bv_ref, g_ref, lng_ref, lnb_ref, o_ref, *, nck):
    L = R_CHUNK
    J = [(c, h) for c in range(nck) for h in range(R_HEADS)]
    rsl = lambda c: slice(c * L, (c + 1) * L)
    hsl = lambda h: slice(h * R_DH, (h + 1) * R_DH)
    sp = [_split(s_ref[0, c, h]) for c, h in J]
    rp = [_split(rq_ref[0, rsl(c), hsl(h)]) for c, h in J]
    y = [y0_ref[0, rsl(c), hsl(h)] + _dot_nt(rp[j][0], sp[j][0])
         + (_dot_nt(rp[j][1], sp[j][0]) + _dot_nt(rp[j][0], sp[j][1])) for j, (c, h) in enumerate(J)]
    for j, (c, h) in enumerate(J):
        rs, sl = rsl(c), hsl(h)
        mu = jnp.mean(y[j], axis=1, keepdims=True)
        yc = y[j] - mu
        var = jnp.mean(yc * yc, axis=1, keepdims=True)
        yn = yc * lax.rsqrt(var + R_GN_EPS) * lng_ref[:, sl] + lnb_ref[:, sl]
        o_ref[0, rs, sl] = ((yn + bv_ref[0, rs, sl].astype(F32)) * g_ref[0, rs, sl].astype(F32)).astype(o_ref.dtype)


def _rwkv_scan(m, n, rq, y0, bv, g, lng, lnb, nck_state, nck_out):
    bsz, s, w = rq.shape
    L = R_CHUNK
    nc = s // L
    tok = lambda bi, c: (bi, c, 0)
    st = lambda bi, c: (bi, c, 0, 0, 0)
    full = lambda bi, c: (0, 0)
    sblock = lambda k: (1, k, R_HEADS, R_DH, R_DH)
    states = pl.pallas_call(
        functools.partial(_rstate_kernel, nck=nck_state),
        grid=(bsz, nc // nck_state),
        in_specs=[pl.BlockSpec(sblock(nck_state), st), pl.BlockSpec(sblock(nck_state), st)],
        out_specs=pl.BlockSpec(sblock(nck_state), st),
        out_shape=jax.ShapeDtypeStruct(m.shape, F32),
        scratch_shapes=[pltpu.VMEM((R_HEADS, R_DH, R_DH), F32)],
        compiler_params=_cp("parallel", "arbitrary"),
        name="rwkv_state",
    )(m, n)
    k = nck_out
    return pl.pallas_call(
        functools.partial(_rout_kernel, nck=k),
        grid=(bsz, nc // k),
        in_specs=[pl.BlockSpec(sblock(k), st),
                  pl.BlockSpec((1, k * L, w), tok), pl.BlockSpec((1, k * L, w), tok),
                  pl.BlockSpec((1, k * L, w), tok), pl.BlockSpec((1, k * L, w), tok),
                  pl.BlockSpec((1, w), full), pl.BlockSpec((1, w), full)],
        out_specs=pl.BlockSpec((1, k * L, w), tok),
        out_shape=jax.ShapeDtypeStruct((bsz, s, w), BF16),
        compiler_params=_cp("parallel", "parallel"),
        name="rwkv_out",
    )(states, rq, y0, bv, g, lng, lnb)


def _merge_kernel(ha_ref, hb_ref, y_ref, gate_ref, x_ref, mup_ref, fup_ref, rup_ref, wo_ref, g_ref, b_ref,
                  o32_ref, o16_ref):
    d = D_MODEL
    merged = (_sigmoid(gate_ref[:, 0:d].astype(F32)) * _dot(ha_ref[...], mup_ref[...])
              + _sigmoid(gate_ref[:, d:2 * d].astype(F32)) * _dot(hb_ref[...], fup_ref[...])
              + _sigmoid(gate_ref[:, 2 * d:3 * d].astype(F32)) * _dot(y_ref[...], rup_ref[...]))
    mix = _dot(merged.astype(BF16), wo_ref[...])
    out = _ln(ALPHA * x_ref[...] + mix, g_ref[...], b_ref[...])
    o32_ref[...] = out
    o16_ref[...] = out.astype(BF16)


def _merge(ha, hb, y, p16, x32, mup, fup, rup, wo, g, b, tm):
    t, d = x32.shape
    tm = min(tm, t)
    full = lambda i: (0, 0)
    tok = lambda i: (i, 0)
    return pl.pallas_call(
        _merge_kernel,
        grid=(t // tm,),
        in_specs=[pl.BlockSpec((tm, 512), tok), pl.BlockSpec((tm, 512), tok), pl.BlockSpec((tm, 512), tok),
                  pl.BlockSpec((tm, 3 * d), lambda i: (i, C_GATE // (3 * d))),
                  pl.BlockSpec((tm, d), tok),
                  pl.BlockSpec((512, d), full), pl.BlockSpec((512, d), full), pl.BlockSpec((512, d), full),
                  pl.BlockSpec((d, d), full), pl.BlockSpec((1, d), full), pl.BlockSpec((1, d), full)],
        out_specs=[pl.BlockSpec((tm, d), tok), pl.BlockSpec((tm, d), tok)],
        out_shape=[jax.ShapeDtypeStruct((t, d), F32), jax.ShapeDtypeStruct((t, d), BF16)],
        compiler_params=_cp("parallel"),
        name="merge_ln1",
    )(ha, hb, y, p16, x32, mup, fup, rup, wo, g, b)


def _xattn_kernel(x16_ref, x32_ref, k_ref, v_ref, wq_ref, wo_ref, g_ref, b_ref, o32_ref, o16_ref):
    q = _dot(x16_ref[0], wq_ref[...]).astype(BF16)
    outs = []
    for h in range(X_HEADS):
        sl = slice(h * X_DH, (h + 1) * X_DH)
        lg = _dot_nt(q[:, sl], k_ref[0, :, sl]) * (X_DH ** -0.5)
        mx = jnp.max(lg, axis=1, keepdims=True)
        e = jnp.exp(lg - mx)
        p = e / jnp.sum(e, axis=1, keepdims=True)
        outs.append(_dot(p.astype(BF16), v_ref[0, :, sl]))
    o = jnp.concatenate(outs, axis=1).astype(BF16)
    out = _ln(ALPHA * x32_ref[0] + _dot(o, wo_ref[...]), g_ref[...], b_ref[...])
    o32_ref[0] = out
    o16_ref[0] = out.astype(BF16)


def _xattn(x16, x32, kk, vv, wq, wo, g, b, tm):
    bsz, s, d = x32.shape
    tm = min(tm, s)
    m = kk.shape[1]
    full = lambda bi, i: (0, 0)
    tok = lambda bi, i: (bi, i, 0)
    mem = lambda bi, i: (bi, 0, 0)
    return pl.pallas_call(
        _xattn_kernel,
        grid=(bsz, s // tm),
        in_specs=[pl.BlockSpec((1, tm, d), tok), pl.BlockSpec((1, tm, d), tok),
                  pl.BlockSpec((1, m, d), mem), pl.BlockSpec((1, m, d), mem),
                  pl.BlockSpec((d, d), full), pl.BlockSpec((d, d), full),
                  pl.BlockSpec((1, d), full), pl.BlockSpec((1, d), full)],
        out_specs=[pl.BlockSpec((1, tm, d), tok), pl.BlockSpec((1, tm, d), tok)],
        out_shape=[jax.ShapeDtypeStruct((bsz, s, d), F32), jax.ShapeDtypeStruct((bsz, s, d), BF16)],
        compiler_params=_cp("parallel", "parallel"),
        name="xattn_ln2",
    )(x16, x32, kk, vv, wq, wo, g, b)


def _ffn_kernel(x16_ref, x32_ref, wg_ref, wu_ref, wd_ref, g_ref, b_ref, o32_ref, o16_ref, acc_sc):
    j = pl.program_id(1)

    @pl.when(j == 0)
    def _():
        acc_sc[...] = jnp.zeros(acc_sc.shape, F32)

    x = x16_ref[...]
    act = _silu(_dot(x, wg_ref[...])) * _dot(x, wu_ref[...])
    acc_sc[...] += _dot(act.astype(BF16), wd_ref[...])

    @pl.when(j == pl.num_programs(1) - 1)
    def _():
        out = _ln(ALPHA * x32_ref[...] + acc_sc[...], g_ref[...], b_ref[...])
        o32_ref[...] = out
        o16_ref[...] = out.astype(BF16)


def _ffn(x16, x32, wgu, wd, g, b, tm, bf):
    t, d = x32.shape
    tm = min(tm, t)
    dff = wd.shape[0]
    nf = dff // bf
    tok = lambda i, j: (i, 0)
    full = lambda i, j: (0, 0)
    return pl.pallas_call(
        _ffn_kernel,
        grid=(t // tm, nf),
        in_specs=[pl.BlockSpec((tm, d), tok), pl.BlockSpec((tm, d), tok),
                  pl.BlockSpec((d, bf), lambda i, j: (0, j)),
                  pl.BlockSpec((d, bf), lambda i, j: (0, j + nf)),
                  pl.BlockSpec((bf, d), lambda i, j: (j, 0)),
                  pl.BlockSpec((1, d), full), pl.BlockSpec((1, d), full)],
        out_specs=[pl.BlockSpec((tm, d), tok), pl.BlockSpec((tm, d), tok)],
        out_shape=[jax.ShapeDtypeStruct((t, d), F32), jax.ShapeDtypeStruct((t, d), BF16)],
        scratch_shapes=[pltpu.VMEM((tm, d), F32)],
        compiler_params=_cp("parallel", "arbitrary"),
        name="ffn_ln3",
    )(x16, x32, wgu, wgu, wd, g, b)


RT_E1, RT_E2, RT_W1, RT_W2, RT_R1, RT_R2 = 0, 1, 2, 3, 4, 5


def _route_kernel(x_ref, rw_ref, rb_ref, info_ref, cnt_ref, carry_sc, *, tr):
    @pl.when(pl.program_id(0) == 0)
    def _():
        carry_sc[...] = jnp.zeros(carry_sc.shape, F32)

    lane = lax.broadcasted_iota(jnp.int32, (tr, 128), 1).astype(F32)
    logits = _dot_x3(x_ref[...], rw_ref[...]) + rb_ref[...]
    logits = jnp.where(lane < N_EXPERTS, logits, -jnp.inf)
    m1 = jnp.max(logits, axis=1, keepdims=True)
    i1 = jnp.min(jnp.where(logits == m1, lane, 128.0), axis=1, keepdims=True)
    rest = jnp.where(lane == i1, -jnp.inf, logits)
    m2 = jnp.max(rest, axis=1, keepdims=True)
    i2 = jnp.min(jnp.where(rest == m2, lane, 128.0), axis=1, keepdims=True)
    e2 = jnp.exp(m2 - m1)
    w1 = 1.0 / (1.0 + e2)
    w2 = e2 / (1.0 + e2)
    oh1 = lane == i1
    oh2 = lane == i2
    oh = jnp.logical_or(oh1, oh2).astype(F32)
    r = lax.broadcasted_iota(jnp.int32, (tr, tr), 0)
    c = lax.broadcasted_iota(jnp.int32, (tr, tr), 1)
    before = (c < r).astype(BF16)
    carry = carry_sc[0:1, :]
    rank = _dot(before, oh.astype(BF16)) + carry
    r1 = jnp.sum(jnp.where(oh1, rank, 0.0), axis=1, keepdims=True)
    r2 = jnp.sum(jnp.where(oh2, rank, 0.0), axis=1, keepdims=True)
    total = carry + jnp.sum(oh, axis=0, keepdims=True)
    carry_sc[...] = jnp.broadcast_to(total, carry_sc.shape)
    cnt_ref[...] = jnp.broadcast_to(total, cnt_ref.shape)
    rec = jnp.zeros((tr, 128), F32)
    for ln, val in ((RT_E1, i1), (RT_E2, i2), (RT_W1, w1), (RT_W2, w2), (RT_R1, r1), (RT_R2, r2)):
        rec = jnp.where(lane == float(ln), val, rec)
    info_ref[...] = rec


def _route(x32, rw, rb, tr):
    t, d = x32.shape
    tr = min(tr, t)
    full = lambda i: (0, 0)
    return pl.pallas_call(
        functools.partial(_route_kernel, tr=tr),
        grid=(t // tr,),
        in_specs=[pl.BlockSpec((tr, d), lambda i: (i, 0)), pl.BlockSpec((d, 128), full), pl.BlockSpec((1, 128), full)],
        out_specs=[pl.BlockSpec((tr, 128), lambda i: (i, 0)), pl.BlockSpec((8, 128), full)],
        out_shape=[jax.ShapeDtypeStruct((t, 128), F32), jax.ShapeDtypeStruct((8, 128), F32)],
        scratch_shapes=[pltpu.VMEM((8, 128), F32)],
        compiler_params=_cp("arbitrary"),
        name="moe_route",
    )(x32, rw, rb)


def _experts_kernel(te_ref, nu_ref, src_ref, srcn_ref, dst_ref, x_hbm, wg_ref, wu_ref, wd_ref, yinit_hbm, yg_hbm,
                    xbuf, x16_sc, acc_sc, ybuf, gsem, ssem, *, tm):
    del te_ref, yinit_hbm
    i = pl.program_id(0)
    j = pl.program_id(1)
    nf = pl.num_programs(1)
    nu = nu_ref[0]
    slot = i % 2

    def gather(idx_ref, s):
        return lambda r: pltpu.make_async_copy(x_hbm.at[pl.ds(idx_ref[0, 0, r], 1)], xbuf.at[s, pl.ds(r, 1)], gsem.at[s])

    def scatter(s):
        return lambda r: pltpu.make_async_copy(ybuf.at[s, pl.ds(r, 1)], yg_hbm.at[pl.ds(dst_ref[0, 0, r], 1)], ssem.at[s])

    def start_all(copy):
        lax.fori_loop(0, tm, lambda r, c: (copy(r).start(), c)[1], 0)

    def wait_all(copy):
        lax.fori_loop(0, tm, lambda r, c: (copy(r).wait(), c)[1], 0)

    @pl.when(i < nu)
    def _():
        @pl.when(j == 0)
        def _():
            @pl.when(i == 0)
            def _():
                start_all(gather(src_ref, slot))
            wait_all(gather(src_ref, slot))
            x16_sc[...] = xbuf[slot].astype(BF16)
            acc_sc[...] = jnp.zeros(acc_sc.shape, F32)

        @pl.when(jnp.logical_and(j == 1, i + 1 < nu))
        def _():
            start_all(gather(srcn_ref, 1 - slot))

        x = x16_sc[...]
        act = _silu(_dot(x, wg_ref[0])) * _dot(x, wu_ref[0])
        acc_sc[...] += _dot(act.astype(BF16), wd_ref[0])

        @pl.when(j == nf - 1)
        def _():
            @pl.when(i > 0)
            def _():
                wait_all(scatter(1 - slot))
            ybuf[slot] = acc_sc[...]
            start_all(scatter(slot))

            @pl.when(i == nu - 1)
            def _():
                wait_all(scatter(slot))


def _experts(tile_expert, n_used, src_row, dst_row, x32, wgu, wd, tm, bf):
    rtot = src_row.shape[0]
    d = x32.shape[1]
    ne, dff, _ = wd.shape
    nf = dff // bf
    nt = rtot // tm
    assert nf >= 2
    ex = lambda i, te, nu: te[jnp.minimum(i, nu[0] - 1)]
    idx = lambda f: pl.BlockSpec((1, 1, tm), lambda i, j, te, nu: (f(i, nu), 0, 0), memory_space=pltpu.SMEM)
    cur = lambda i, nu: jnp.minimum(i, nu[0] - 1)
    nxt = lambda i, nu: jnp.minimum(i + 1, nu[0] - 1)
    src3 = src_row.reshape(nt, 1, tm)
    return pl.pallas_call(
        functools.partial(_experts_kernel, tm=tm),
        grid_spec=pltpu.PrefetchScalarGridSpec(
            num_scalar_prefetch=2,
            grid=(nt, nf),
            in_specs=[idx(cur), idx(nxt), idx(cur),
                      pl.BlockSpec(memory_space=pl.ANY),
                      pl.BlockSpec((1, d, bf), lambda i, j, te, nu: (ex(i, te, nu), 0, j)),
                      pl.BlockSpec((1, d, bf), lambda i, j, te, nu: (ex(i, te, nu), 0, j + nf)),
                      pl.BlockSpec((1, bf, d), lambda i, j, te, nu: (ex(i, te, nu), j, 0)),
                      pl.BlockSpec(memory_space=pl.ANY)],
            out_specs=pl.BlockSpec(memory_space=pl.ANY),
            scratch_shapes=[pltpu.VMEM((2, tm, d), F32), pltpu.VMEM((tm, d), BF16), pltpu.VMEM((tm, d), F32),
                            pltpu.VMEM((2, tm, d), F32), pltpu.SemaphoreType.DMA((2,)), pltpu.SemaphoreType.DMA((2,))]),
        out_shape=jax.ShapeDtypeStruct((rtot, d), F32),
        input_output_aliases={9: 0},
        compiler_params=_cp("arbitrary", "arbitrary"),
        name="moe_experts",
    )(tile_expert, n_used, src3, src3, dst_row.reshape(nt, 1, tm), x32, wgu, wgu, wd, jnp.zeros((rtot, d), F32))


def _combine_kernel(x_ref, y1_ref, y2_ref, info_ref, g_ref, b_ref, o_ref):
    info = info_ref[...]
    w1 = info[:, RT_W1:RT_W1 + 1]
    w2 = info[:, RT_W2:RT_W2 + 1]
    o_ref[...] = _ln(ALPHA * x_ref[...] + (w1 * y1_ref[...] + w2 * y2_ref[...]), g_ref[...], b_ref[...])


def _combine(x32, yg, info, g, b, tm):
    t, d = x32.shape
    tm = min(tm, t)
    nt = t // tm
    tok = lambda i: (i, 0)
    full = lambda i: (0, 0)
    return pl.pallas_call(
        _combine_kernel,
        grid=(nt,),
        in_specs=[pl.BlockSpec((tm, d), tok), pl.BlockSpec((tm, d), tok), pl.BlockSpec((tm, d), lambda i: (i + nt, 0)),
                  pl.BlockSpec((tm, 128), tok), pl.BlockSpec((1, d), full), pl.BlockSpec((1, d), full)],
        out_specs=pl.BlockSpec((tm, d), tok),
        out_shape=jax.ShapeDtypeStruct((t, d), F32),
        compiler_params=_cp("parallel"),
        name="moe_combine_ln3",
    )(x32, yg, yg, info, g, b)


def _moe(x32, rw, rb, wgu, wd, g, b, tm, bf):
    t, d = x32.shape
    tm = min(tm, t)
    info, cnt = _route(x32, rw, rb, 512)
    i32 = jnp.int32
    e1, e2 = info[:, RT_E1].astype(i32), info[:, RT_E2].astype(i32)
    r1, r2 = info[:, RT_R1].astype(i32), info[:, RT_R2].astype(i32)
    tiles_e = (cnt[0, :N_EXPERTS].astype(i32) + tm - 1) // tm
    tile_end = jnp.cumsum(tiles_e)
    start = (tile_end - tiles_e) * tm
    nt = (2 * t) // tm + N_EXPERTS
    n_used = tile_end[N_EXPERTS - 1:]
    tile_expert = jnp.minimum(jnp.searchsorted(tile_end, jnp.arange(nt, dtype=i32), side='right'),
                              N_EXPERTS - 1).astype(i32)
    pos = jnp.concatenate([start[e1] + r1, start[e2] + r2])
    tok = jnp.arange(t, dtype=i32)
    rows = nt * tm
    src_row = jnp.zeros((rows,), i32).at[pos].set(jnp.concatenate([tok, tok]))
    slot_row = jnp.full((rows,), -1, i32).at[pos].set(jnp.arange(2 * t, dtype=i32))
    pad = slot_row < 0
    dst_row = jnp.where(pad, 2 * t + jnp.cumsum(pad.astype(i32)) - 1, slot_row)
    yg = _experts(tile_expert, n_used, src_row, dst_row, x32, wgu, wd, tm, bf)
    return _combine(x32, yg, info, g, b, tm)


def _src_layout(with_vres):
    cols = [('m_qk', 512), ('m_v', 512), ('m_o', 512), ('m_i', 4), ('m_f', 4),
            ('f_q', 512), ('f_k', 512), ('f_v', 512), ('f_f', 8), ('gate', 3072),
            ('r_r', 512), ('r_k', 512), ('r_v', 512), ('r_w', 64), ('r_a', 64), ('r_g', 128)]
    if with_vres:
        cols.append(('r_vres', 32))
    layout, start = {}, 0
    for name, width in cols:
        layout[name] = (start, start + width)
        start += width
    return layout


def _relayout_cols(w, with_vres):
    lay = _src_layout(with_vres)
    lead = w.shape[:-1]
    z = lambda n: jnp.zeros(lead + (n,), w.dtype)
    c = lambda name: w[..., lay[name][0]:lay[name][1]]

    def slots(x, scale):
        x = x.reshape(lead + (F_HEADS, F_DH)) * scale
        return jnp.concatenate([x, jnp.zeros_like(x)], axis=-1).reshape(lead + (F_HEADS * F_PAD,))

    parts = [c('gate'), c('r_r'), c('r_k'), c('r_v'),
             c('r_w'), c('r_a'), c('r_g'),
             c('m_i'), c('m_f'), c('f_f'), z(16), c('r_vres') if with_vres else z(32), z(64), z(128),
             slots(c('f_q'), F_DH ** -0.5 * LOG2E), slots(c('f_k'), 1.0),
             c('m_qk'), c('m_v'), c('m_o'), c('f_v')]
    return jnp.concatenate(parts, axis=-1)


def _row(v):
    return v.reshape(1, -1).astype(F32)


def _branches(x16, bsz, s, prm, v_first):
    (w_in, b_in, m_conv, m_norm, m_up, f_up, r_mu, r_wbias, r_wB, r_abias, r_aB, r_vbias, r_vB,
     r_gB, r_kk, r_ka, r_rk, r_ln_g, r_ln_b, r_up, w_out, ln1_g, ln1_b) = prm
    with_vres = r_vbias is not None
    w_all = _relayout_cols(w_in, with_vres)
    b_all = _relayout_cols(b_in, with_vres).reshape(1, NP)
    p16 = _matmul_bias(x16, w_all.astype(BF16), b_all, BF16, 1024, 1024, "in_proj")
    gates = _matmul_bias(x16, w_all[:, C_SMALL:C_SMALL + 128].astype(BF16), b_all[:, C_SMALL:C_SMALL + 128],
                         F32, 2048, 128, "in_proj_gates")
    p16 = p16.reshape(bsz, s, NP)
    gates = gates.reshape(bsz, s, 128)
    gates_t = jnp.transpose(gates, (0, 2, 1))

    h_a = _mlstm(p16, gates, gates_t, m_conv.astype(F32), _row(m_norm), 256)
    q_aug, k_aug, v_aug = _fox_prep(p16, gates_t, 512)
    h_b = _fox_attention(q_aug, k_aug, v_aug, 512)
    lay = _src_layout(with_vres)
    r0 = lay['r_r'][0]
    mu = r_mu[:1536].reshape(1, 1536)
    mum = jnp.concatenate([r_mu[1536:1792], jnp.zeros((32,), F32),
                           r_mu[lay['r_vres'][0] - r0:] if with_vres else jnp.zeros((32,), F32),
                           jnp.zeros((192,), F32)]).reshape(1, 512)
    z = jnp.zeros((64, 512), F32)
    w_wa = jnp.concatenate([jnp.concatenate([r_wB, z], axis=1), jnp.concatenate([z, r_aB], axis=1)], axis=0).astype(BF16)
    w_v = None
    if with_vres:
        w_v = jnp.concatenate([jnp.zeros((32, 512), F32), r_vB, jnp.zeros((64, 512), F32)], axis=0).astype(BF16)
    hid = jnp.arange(512) // R_DH
    headsum = (hid[:, None] == hid[None, :]).astype(BF16)
    outs = _rwkv_prep(p16, v_first, mu, mum, w_wa, w_v, r_gB.astype(BF16), headsum, _row(r_wbias), _row(r_abias),
                      _row(r_vbias) if with_vres else None, _row(r_kk), _row(r_ka), _row(r_rk), 256)
    r, lw, k2, v, kk, b, g, bv = outs[:8]
    v_own = v_first if with_vres else outs[8]
    m_c, n_c, rq, y0 = _rwkv_chunks(r, lw, k2, v, kk, b, 2)
    y = _rwkv_scan(m_c, n_c, rq, y0, bv, g, _row(r_ln_g), _row(r_ln_b), 8, 2)
    return p16, h_a, h_b, y, v_own


def _token_mixer(x16, x32, bsz, s, prm, v_first):
    m_up, f_up, r_up, w_out, ln1_g, ln1_b = prm[4], prm[5], prm[19], prm[20], prm[21], prm[22]
    t = bsz * s
    p16, h_a, h_b, y, v_own = _branches(x16, bsz, s, prm, v_first)
    x1_32, x1_16 = _merge(h_a.reshape(t, 512), h_b.reshape(t, 512), y.reshape(t, 512), p16.reshape(t, NP), x32,
                          m_up.astype(BF16), f_up.astype(BF16), r_up.astype(BF16), w_out.astype(BF16),
                          _row(ln1_g), _row(ln1_b), 512)
    return x1_32, x1_16, v_own


def kernel(x, mem, mem_ln_g, mem_ln_b, w_in_0, b_in_0, m_conv_0, m_norm_0, m_up_0, f_up_0, r_mu_0, r_wbias_0, r_wB_0, r_abias_0, r_aB_0, r_gB_0, r_kk_0, r_ka_0, r_rk_0, r_ln_g_0, r_ln_b_0, r_up_0, w_out_0, ln1_g_0, ln1_b_0, x_wq_0, x_wkv_0, x_wo_0, ln2_g_0, ln2_b_0, ff_wgu_0, ff_wd_0, ln3_g_0, ln3_b_0, w_in_1, b_in_1, m_conv_1, m_norm_1, m_up_1, f_up_1, r_mu_1, r_wbias_1, r_wB_1, r_abias_1, r_aB_1, r_vbias_1, r_vB_1, r_gB_1, r_kk_1, r_ka_1, r_rk_1, r_ln_g_1, r_ln_b_1, r_up_1, w_out_1, ln1_g_1, ln1_b_1, x_wq_1, x_wkv_1, x_wo_1, ln2_g_1, ln2_b_1, ex_router_1, ex_router_b_1, ex_wgu_1, ex_wd_1, ln3_g_1, ln3_b_1):
    bsz, s, d = x.shape
    t = bsz * s
    x32 = x.reshape(t, d)
    x16 = x32.astype(BF16)

    mixers = (
        (w_in_0, b_in_0, m_conv_0, m_norm_0, m_up_0, f_up_0, r_mu_0, r_wbias_0, r_wB_0, r_abias_0, r_aB_0, None, None,
         r_gB_0, r_kk_0, r_ka_0, r_rk_0, r_ln_g_0, r_ln_b_0, r_up_0, w_out_0, ln1_g_0, ln1_b_0),
        (w_in_1, b_in_1, m_conv_1, m_norm_1, m_up_1, f_up_1, r_mu_1, r_wbias_1, r_wB_1, r_abias_1, r_aB_1, r_vbias_1, r_vB_1,
         r_gB_1, r_kk_1, r_ka_1, r_rk_1, r_ln_g_1, r_ln_b_1, r_up_1, w_out_1, ln1_g_1, ln1_b_1),
    )
    xattn = ((x_wq_0, x_wo_0, ln2_g_0, ln2_b_0), (x_wq_1, x_wo_1, ln2_g_1, ln2_b_1))
    kv0, kv1 = _mem_kv(mem, _row(mem_ln_g), _row(mem_ln_b), x_wkv_0.astype(BF16), x_wkv_1.astype(BF16))
    kvs = (kv0, kv1)

    v_first = None
    for l in range(DEPTH):
        x32, x16, v_own = _token_mixer(x16, x32, bsz, s, mixers[l], v_first)
        if l == 0:
            v_first = v_own
        wq, wo, g2, b2 = xattn[l]
        kv = kvs[l]
        x32, x16 = _xattn(x16.reshape(bsz, s, d), x32.reshape(bsz, s, d), kv[:, :, :d], kv[:, :, d:],
                          wq.astype(BF16), wo.astype(BF16), _row(g2), _row(b2), 512)
        x32, x16 = x32.reshape(t, d), x16.reshape(t, d)
        if l % 2 == 0:
            x32, x16 = _ffn(x16, x32, ff_wgu_0.astype(BF16), ff_wd_0.astype(BF16), _row(ln3_g_0), _row(ln3_b_0), 512, 1408)
        else:
            rw = jnp.concatenate([ex_router_1, jnp.zeros((d, 128 - N_EXPERTS), F32)], axis=1)
            rb = jnp.concatenate([ex_router_b_1, jnp.zeros((128 - N_EXPERTS,), F32)]).reshape(1, 128)
            x32 = _moe(x32, rw, rb, ex_wgu_1.astype(BF16), ex_wd_1.astype(BF16), _row(ln3_g_1), _row(ln3_b_1), 1024, 896)
    return x32.reshape(bsz, s, d)
```

```python
import functools

import jax
import jax.numpy as jnp
from jax import lax
from jax.experimental import pallas as pl
from jax.experimental.pallas import tpu as pltpu

F32 = jnp.float32
BF16 = jnp.bfloat16

D_MODEL = 1024
DEPTH = 2
M_HEADS, M_DV, M_DQK, M_CONV, M_CHUNK = 4, 128, 64, 4, 64
F_HEADS, F_DH, F_PAD = 8, 64, 128
R_HEADS, R_DH, R_CHUNK = 8, 64, 64
R_GN_EPS = 64e-5
X_HEADS, X_DH = 4, 256
N_EXPERTS = 8
ALPHA = (2.0 * DEPTH) ** 0.25
LN_EPS = 1e-5
HEAD_NORM_EPS = 1e-6
NEG = -1e30
LOG2E = 1.4426950408889634

C_GATE = 0
C_RKV = 3072
C_RMISC = 4608
C_SMALL = 4864
C_FQ = 5120
C_FK = 6144
C_MQK = 7168
C_MV = 7680
C_MO = 8192
C_FV = 8704
NP = 9216
VMEM_LIMIT = 56 * 1024 * 1024


def _cp(*sem):
    return pltpu.CompilerParams(dimension_semantics=sem, vmem_limit_bytes=VMEM_LIMIT)


def _dot(a, b):
    return jnp.dot(a, b, preferred_element_type=F32)


def _dot_nt(a, b):
    return lax.dot_general(a, b, (((1,), (1,)), ((), ())), preferred_element_type=F32)


def _dot_tn(a, b):
    return lax.dot_general(a, b, (((0,), (0,)), ((), ())), preferred_element_type=F32)


def _split(x):
    hi = x.astype(BF16)
    lo = (x - hi.astype(F32)).astype(BF16)
    return hi, lo


def _split3(x):
    h1 = x.astype(BF16)
    r1 = x - h1.astype(F32)
    h2 = r1.astype(BF16)
    h3 = (r1 - h2.astype(F32)).astype(BF16)
    return h1, h2, h3


def _dot_x2(a, b16):
    hi, lo = _split(a)
    return _dot(hi, b16) + _dot(lo, b16)


def _dot_x3(a, b):
    ah, al = _split(a)
    bh, bl = _split(b)
    return _dot(ah, bh) + (_dot(al, bh) + _dot(ah, bl))


def _dot16(a, b):
    return _dot(a.astype(BF16), b.astype(BF16))


def _sigmoid(x):
    return 1.0 / (1.0 + jnp.exp(-x))


def _log_sigmoid(x):
    return jnp.minimum(x, 0.0) - jnp.log(1.0 + jnp.exp(-jnp.abs(x)))


def _softplus(x):
    return jnp.maximum(x, 0.0) + jnp.log(1.0 + jnp.exp(-jnp.abs(x)))


def _silu(x):
    return x * _sigmoid(x)


def _ln(z, g, b, eps=LN_EPS):
    mu = jnp.mean(z, axis=-1, keepdims=True)
    zc = z - mu
    var = jnp.mean(zc * zc, axis=-1, keepdims=True)
    return zc * lax.rsqrt(var + eps) * g + b


def _mm_kernel(a_ref, w_ref, b_ref, o_ref):
    o_ref[...] = (_dot(a_ref[...], w_ref[...]) + b_ref[...]).astype(o_ref.dtype)


def _matmul_bias(a, w, b, out_dtype, tm, tn, name):
    m, k = a.shape
    n = w.shape[1]
    tm, tn = min(tm, m), min(tn, n)
    return pl.pallas_call(
        _mm_kernel,
        grid=(m // tm, n // tn),
        in_specs=[pl.BlockSpec((tm, k), lambda i, j: (i, 0)),
                  pl.BlockSpec((k, tn), lambda i, j: (0, j)),
                  pl.BlockSpec((1, tn), lambda i, j: (0, j))],
        out_specs=pl.BlockSpec((tm, tn), lambda i, j: (i, j)),
        out_shape=jax.ShapeDtypeStruct((m, n), out_dtype),
        compiler_params=_cp("parallel", "parallel"),
        name=name,
    )(a, w, b)


def _memkv_kernel(mem_ref, g_ref, b_ref, w0_ref, w1_ref, o0_ref, o1_ref):
    mn = _ln(mem_ref[0], g_ref[...], b_ref[...]).astype(BF16)
    o0_ref[0] = _dot(mn, w0_ref[...]).astype(BF16)
    o1_ref[0] = _dot(mn, w1_ref[...]).astype(BF16)


def _mem_kv(mem, g, b, wkv0, wkv1):
    bsz, m, d = mem.shape
    n = wkv0.shape[1]
    full = lambda i: (0, 0)
    return pl.pallas_call(
        _memkv_kernel,
        grid=(bsz,),
        in_specs=[pl.BlockSpec((1, m, d), lambda i: (i, 0, 0)),
                  pl.BlockSpec((1, d), full), pl.BlockSpec((1, d), full),
                  pl.BlockSpec((d, n), full), pl.BlockSpec((d, n), full)],
        out_specs=[pl.BlockSpec((1, m, n), lambda i: (i, 0, 0))] * 2,
        out_shape=[jax.ShapeDtypeStruct((bsz, m, n), BF16)] * 2,
        compiler_params=_cp("parallel"),
        name="mem_kv",
    )(mem, g, b, wkv0, wkv1)


def _foxprep_kernel(g_ref, q_ref, k_ref, v_ref, qo_ref, ko_ref, vo_ref, carry_sc, *, tt):
    blk = 128

    @pl.when(pl.program_id(1) == 0)
    def _():
        carry_sc[...] = jnp.zeros(carry_sc.shape, F32)

    r = lax.broadcasted_iota(jnp.int32, (blk, blk), 0)
    c = lax.broadcasted_iota(jnp.int32, (blk, blk), 1)
    upper = (r <= c).astype(BF16)
    carry = carry_sc[:, 0:1]
    parts = []
    for i in range(tt // blk):
        ls = _log_sigmoid(g_ref[0, :, i * blk:(i + 1) * blk])
        h1, h2, h3 = _split3(ls)
        cs = _dot(h1, upper) + (_dot(h2, upper) + _dot(h3, upper)) + carry
        carry = cs[:, blk - 1:blk]
        parts.append(cs)
    carry_sc[...] = jnp.broadcast_to(carry, carry_sc.shape)
    cum = jnp.concatenate(parts, axis=1) * LOG2E
    c1, c2, c3 = [p.astype(F32) for p in _split3(cum)]
    w = F_HEADS * F_PAD
    row = lax.broadcasted_iota(jnp.int32, (F_HEADS, w), 0)
    lane = lax.broadcasted_iota(jnp.int32, (F_HEADS, w), 1)
    sel = lambda j: (lane == row * F_PAD + F_DH + j).astype(F32)
    lane1 = lax.broadcasted_iota(jnp.int32, (1, w), 1) % F_PAD
    ones_q = jnp.logical_and(lane1 >= F_DH + 3, lane1 < F_DH + 6).astype(F32)
    ones_k = jnp.logical_and(lane1 >= F_DH, lane1 < F_DH + 3).astype(F32)
    aug_q = _dot_tn(c1, sel(0)) + _dot_tn(c2, sel(1)) + _dot_tn(c3, sel(2)) + ones_q
    aug_k = ones_k - (_dot_tn(c1, sel(3)) + _dot_tn(c2, sel(4)) + _dot_tn(c3, sel(5)))
    qo_ref[0] = (q_ref[0].astype(F32) + aug_q).astype(BF16)
    ko_ref[0] = (k_ref[0].astype(F32) + aug_k).astype(BF16)
    wv = F_HEADS * F_DH
    src = lax.broadcasted_iota(jnp.int32, (wv, w), 0)
    dst = lax.broadcasted_iota(jnp.int32, (wv, w), 1)
    place = (dst == (src // F_DH) * F_PAD + src % F_DH).astype(BF16)
    ones_v = (lane1 >= F_DH).astype(F32)
    vo_ref[0] = jnp.transpose(_dot(v_ref[0], place) + ones_v).astype(BF16)


def _fox_prep(p16, gates_t, tt):
    bsz, s, _ = p16.shape
    tt = min(tt, s)
    w = F_HEADS * F_PAD
    wv = F_HEADS * F_DH
    tok = lambda b, i: (b, i, 0)
    return pl.pallas_call(
        functools.partial(_foxprep_kernel, tt=tt),
        grid=(bsz, s // tt),
        in_specs=[pl.BlockSpec((1, 8, tt), lambda b, i: (b, 1, i)),
                  pl.BlockSpec((1, tt, w), lambda b, i: (b, i, C_FQ // w)),
                  pl.BlockSpec((1, tt, w), lambda b, i: (b, i, C_FK // w)),
                  pl.BlockSpec((1, tt, wv), lambda b, i: (b, i, C_FV // wv))],
        out_specs=[pl.BlockSpec((1, tt, w), tok), pl.BlockSpec((1, tt, w), tok),
                   pl.BlockSpec((1, w, tt), lambda b, i: (b, 0, i))],
        out_shape=[jax.ShapeDtypeStruct((bsz, s, w), BF16), jax.ShapeDtypeStruct((bsz, s, w), BF16),
                   jax.ShapeDtypeStruct((bsz, w, s), BF16)],
        scratch_shapes=[pltpu.VMEM((8, 128), F32)],
        compiler_params=_cp("parallel", "arbitrary"),
        name="fox_prep",
    )(gates_t, p16, p16, p16)


def _fox_kernel(qt_ref, kt_ref, q_ref, k_ref, v_ref, o_ref, m_sc, acc_sc, *, t):
    qi = qt_ref[pl.program_id(1)]
    ki = kt_ref[pl.program_id(1)]

    @pl.when(ki == 0)
    def _():
        m_sc[...] = jnp.full(m_sc.shape, NEG, F32)
        acc_sc[...] = jnp.zeros(acc_sc.shape, F32)

    def block(diag):
        if diag:
            key = lax.broadcasted_iota(jnp.int32, (t, t), 0)
            qry = lax.broadcasted_iota(jnp.int32, (t, t), 1)
            mask = key <= qry
        pss = [slice(h * F_PAD, (h + 1) * F_PAD) for h in range(F_HEADS)]
        scores = lambda h: _dot_nt(k_ref[0, :, pss[h]], q_ref[0, :, pss[h]])
        s_next = scores(0)
        for h in range(F_HEADS):
            s = s_next
            if h + 1 < F_HEADS:
                s_next = scores(h + 1)
            if diag:
                s = jnp.where(mask, s, NEG)
            m_old = m_sc[h]
            m_new = jnp.maximum(m_old, jnp.max(s, axis=0, keepdims=True))
            p = jnp.exp2(s - m_new).astype(BF16)
            acc_sc[h] = jnp.exp2(m_old - m_new) * acc_sc[h] + _dot(v_ref[0, pss[h], :], p)
            m_sc[h] = m_new

    @pl.when(ki < qi)
    def _():
        block(False)

    @pl.when(ki == qi)
    def _():
        block(True)
        for h in range(F_HEADS):
            acc = acc_sc[h]
            out_t = acc[:F_DH, :] / acc[F_DH:F_DH + 1, :]
            o_ref[0, :, h * F_DH:(h + 1) * F_DH] = jnp.transpose(out_t).astype(o_ref.dtype)


def _fox_attention(q_aug, k_aug, v_aug_t, t):
    bsz, s, w = q_aug.shape
    t = min(t, s)
    n = s // t
    wv = F_HEADS * F_DH
    pairs = [(i, j) for i in range(n) for j in range(i + 1)]
    qt = jnp.asarray([p[0] for p in pairs], jnp.int32)
    kt = jnp.asarray([p[1] for p in pairs], jnp.int32)
    return pl.pallas_call(
        functools.partial(_fox_kernel, t=t),
        grid_spec=pltpu.PrefetchScalarGridSpec(
            num_scalar_prefetch=2,
            grid=(bsz, len(pairs)),
            in_specs=[pl.BlockSpec((1, t, w), lambda b, p, qt, kt: (b, qt[p], 0)),
                      pl.BlockSpec((1, t, w), lambda b, p, qt, kt: (b, kt[p], 0)),
                      pl.BlockSpec((1, w, t), lambda b, p, qt, kt: (b, 0, kt[p]))],
            out_specs=pl.BlockSpec((1, t, wv), lambda b, p, qt, kt: (b, qt[p], 0)),
            scratch_shapes=[pltpu.VMEM((F_HEADS, 1, t), F32), pltpu.VMEM((F_HEADS, F_PAD, t), F32)]),
        out_shape=jax.ShapeDtypeStruct((bsz, s, wv), BF16),
        compiler_params=_cp("parallel", "arbitrary"),
        name="fox_attention",
    )(qt, kt, q_aug, k_aug, v_aug_t)


def _mlstm_kernel(qk_ref, v_ref, o_ref, g_ref, gt_ref, cw_ref, nw_ref, out_ref,
                  prev_sc, q_sc, k_sc, c_sc, m_sc, *, lt):
    ti = pl.program_id(1)
    L = M_CHUNK
    H = range(M_HEADS)

    @pl.when(ti == 0)
    def _():
        prev_sc[...] = jnp.zeros(prev_sc.shape, F32)
        c_sc[...] = jnp.zeros(c_sc.shape, F32)
        m_sc[...] = jnp.zeros(m_sc.shape, F32)

    u = qk_ref[0].astype(F32)
    prev = prev_sc[...]
    rows = lax.broadcasted_iota(jnp.int32, u.shape, 0)
    acc = u * cw_ref[M_CONV - 1:M_CONV, :]
    for d in range(1, M_CONV):
        sh = jnp.where(rows < d, pltpu.roll(prev, d, 0), pltpu.roll(u, d, 0))
        acc = acc + sh * cw_ref[M_CONV - 1 - d:M_CONV - d, :]
    prev_sc[...] = u
    qk = _silu(acc)
    nq = M_HEADS * M_DQK
    q_sc[...] = qk[:, :nq].astype(BF16)
    k_sc[...] = qk[:, nq:] * (M_DQK ** -0.5)

    r = lax.broadcasted_iota(jnp.int32, (L, L), 0)
    c = lax.broadcasted_iota(jnp.int32, (L, L), 1)
    tri = c <= r
    lower = tri.astype(BF16)
    upper = (r <= c).astype(BF16)
    ones = jnp.ones((L, M_DV), BF16)

    for ch in range(lt // L):
        rs = slice(ch * L, (ch + 1) * L)
        gc = g_ref[0, rs, :]
        gr = gt_ref[0, :, rs]
        lc1, lc2 = _split(_log_sigmoid(gc))
        lr1, lr2 = _split(_log_sigmoid(gr))
        bcol = _dot(lower, lc1) + _dot(lower, lc2)
        brow = _dot(lr1, upper) + _dot(lr2, upper)
        hsl = [slice(h * M_DV, (h + 1) * M_DV) for h in H]
        qsl = [slice(h * M_DQK, (h + 1) * M_DQK) for h in H]
        bc = [bcol[:, 4 + h:5 + h] for h in H]
        m_prev = [m_sc[h:h + 1, 0:1] for h in H]
        qh = [q_sc[rs, qsl[h]] for h in H]
        kh = [k_sc[rs, qsl[h]] for h in H]
        vext = [jnp.concatenate([v_ref[0, rs, hsl[h]], ones], axis=1) for h in H]
        cext = [c_sc[h] for h in H]
        dm = [jnp.where(tri, bc[h] - brow[4 + h:5 + h, :] + gr[h:h + 1, :], -jnp.inf) for h in H]
        d_inter = [bc[h] + m_prev[h] for h in H]
        m_t = [jnp.maximum(d_inter[h], jnp.max(dm[h], axis=1, keepdims=True)) for h in H]
        qk_s = [_dot_nt(qh[h], kh[h].astype(BF16)) for h in H]
        qc = [_dot(qh[h], cext[h].astype(BF16)) for h in H]
        w_intra = [(qk_s[h] * jnp.exp(dm[h] - m_t[h])).astype(BF16) for h in H]
        ext = [jnp.exp(d_inter[h] - m_t[h]) * qc[h] + _dot(w_intra[h], vext[h]) for h in H]
        b_last = [bc[h][L - 1:L, :] for h in H]
        g = [b_last[h] - bc[h] + gc[:, h:h + 1] for h in H]
        m_new = [jnp.maximum(b_last[h] + m_prev[h], jnp.max(g[h], axis=0, keepdims=True)) for h in H]
        wk = [(jnp.exp(g[h] - m_new[h]) * kh[h]).astype(BF16) for h in H]
        upd = [_dot_tn(wk[h], vext[h]) for h in H]
        for h in H:
            c_sc[h] = jnp.exp(b_last[h] + m_prev[h] - m_new[h]) * cext[h] + upd[h]
            m_sc[h:h + 1, :] = jnp.broadcast_to(m_new[h], (1, m_sc.shape[1]))
        for h in H:
            hh = ext[h][:, :M_DV] / jnp.maximum(jnp.abs(ext[h][:, M_DV:M_DV + 1]), jnp.exp(-m_t[h]))
            mu = jnp.mean(hh, axis=1, keepdims=True)
            hc = hh - mu
            var = jnp.mean(hc * hc, axis=1, keepdims=True)
            hn = hc * lax.rsqrt(var + HEAD_NORM_EPS) * nw_ref[:, hsl[h]]
            out_ref[0, rs, hsl[h]] = (hn * _sigmoid(o_ref[0, rs, hsl[h]].astype(F32))).astype(out_ref.dtype)


def _mlstm(p16, gates, gates_t, conv_w, norm_w, lt):
    bsz, s, _ = p16.shape
    lt = min(lt, s)
    w = M_HEADS * M_DV
    full = lambda b, i: (0, 0)
    return pl.pallas_call(
        functools.partial(_mlstm_kernel, lt=lt),
        grid=(bsz, s // lt),
        in_specs=[pl.BlockSpec((1, lt, w), lambda b, i: (b, i, C_MQK // w)),
                  pl.BlockSpec((1, lt, w), lambda b, i: (b, i, C_MV // w)),
                  pl.BlockSpec((1, lt, w), lambda b, i: (b, i, C_MO // w)),
                  pl.BlockSpec((1, lt, 128), lambda b, i: (b, i, 0)),
                  pl.BlockSpec((1, 8, lt), lambda b, i: (b, 0, i)),
                  pl.BlockSpec((M_CONV, w), full),
                  pl.BlockSpec((1, w), full)],
        out_specs=pl.BlockSpec((1, lt, w), lambda b, i: (b, i, 0)),
        out_shape=jax.ShapeDtypeStruct((bsz, s, w), BF16),
        scratch_shapes=[pltpu.VMEM((lt, w), F32),
                        pltpu.VMEM((lt, M_HEADS * M_DQK), BF16), pltpu.VMEM((lt, M_HEADS * M_DQK), F32),
                        pltpu.VMEM((M_HEADS, M_DQK, 2 * M_DV), F32), pltpu.VMEM((8, 128), F32)],
        compiler_params=_cp("parallel", "arbitrary"),
        name="mlstm",
    )(p16, p16, p16, gates, gates_t, conv_w, norm_w)


def _rprep_kernel(*refs, tt, has_vres):
    if has_vres:
        (p_ref, pm_ref, vf_ref, mu_ref, mum_ref, wa_ref, wv_ref, wg_ref, hs_ref, wbias_ref, abias_ref, vbias_ref,
         kkw_ref, kaw_ref, rkw_ref,
         r_o, lw_o, k_o, v_o, kk_o, b_o, g_o, bv_o, carry_sc, carrym_sc) = refs
    else:
        (p_ref, pm_ref, mu_ref, mum_ref, wa_ref, wg_ref, hs_ref, wbias_ref, abias_ref,
         kkw_ref, kaw_ref, rkw_ref,
         r_o, lw_o, k_o, v_o, kk_o, b_o, g_o, bv_o, vown_o, carry_sc, carrym_sc) = refs
    ti = pl.program_id(1)

    @pl.when(ti == 0)
    def _():
        carry_sc[...] = jnp.zeros(carry_sc.shape, F32)
        carrym_sc[...] = jnp.zeros(carrym_sc.shape, F32)

    rows = lax.broadcasted_iota(jnp.int32, (tt, 512), 0)

    def shifted(ref, cref, mref, c0, width):
        x = ref[0, :, c0:c0 + width].astype(F32)
        prev = jnp.where(rows[:, :width] == 0, cref[0:1, c0:c0 + width], pltpu.roll(x, 1, 0))
        return x + mref[:, c0:c0 + width] * (prev - x)

    r = shifted(p_ref, carry_sc, mu_ref, 0, 512)
    k = shifted(p_ref, carry_sc, mu_ref, 512, 512)
    v = shifted(p_ref, carry_sc, mu_ref, 1024, 512)
    wa_in = shifted(pm_ref, carrym_sc, mum_ref, 0, 128)
    g_in = shifted(pm_ref, carrym_sc, mum_ref, 128, 128)
    lane = lax.broadcasted_iota(jnp.int32, (tt, 128), 1)
    wa_act = jnp.where(lane < 64, jnp.tanh(wa_in), wa_in).astype(BF16)
    wa = _dot(wa_act, wa_ref[...])
    w_log = -_softplus(-(wbias_ref[...] + wa[:, :512])) - 0.5
    lw_o[0] = -jnp.exp(w_log)
    a = _sigmoid(abias_ref[...] + wa[:, 512:])
    if has_vres:
        s_in = shifted(pm_ref, carrym_sc, mum_ref, 256, 128)
        mix = _sigmoid(vbias_ref[...] + _dot(s_in.astype(BF16), wv_ref[...]))
        v = v + (vf_ref[0].astype(F32) - v) * mix
    else:
        vown_o[0] = v.astype(vown_o.dtype)
    g_o[0] = _dot(_sigmoid(g_in).astype(BF16), wg_ref[...]).astype(g_o.dtype)
    hs = hs_ref[...]
    u = k * kkw_ref[...]
    nrm = jnp.sqrt(_dot_x2(u * u, hs))
    kk = u / jnp.maximum(nrm, 1e-12)
    k2 = k * (1.0 + (a - 1.0) * kaw_ref[...])
    bonus = _dot_x2(r * k2 * rkw_ref[...], hs)
    r_o[0] = r.astype(r_o.dtype)
    k_o[0] = k2.astype(k_o.dtype)
    v_o[0] = v.astype(v_o.dtype)
    kk_o[0] = kk.astype(kk_o.dtype)
    b_o[0] = (kk * a).astype(b_o.dtype)
    bv_o[0] = (bonus * v).astype(bv_o.dtype)
    carry_sc[0:1, :] = p_ref[0, tt - 1:tt, :].astype(F32)
    carrym_sc[0:1, :] = pm_ref[0, tt - 1:tt, :].astype(F32)


def _rwkv_prep(p16, v_first, mu, mum, w_wa, w_v, w_g, headsum, wbias, abias, vbias, kkw, kaw, rkw, tt):
    bsz, s, _ = p16.shape
    tt = min(tt, s)
    has_vres = v_first is not None
    full = lambda b, i: (0, 0)
    tok = lambda b, i: (b, i, 0)
    vec = pl.BlockSpec((1, 512), full)
    ins = [p16, p16]
    specs = [pl.BlockSpec((1, tt, 1536), lambda b, i: (b, i, C_RKV // 1536)),
             pl.BlockSpec((1, tt, 512), lambda b, i: (b, i, C_RMISC // 512))]
    if has_vres:
        ins.append(v_first)
        specs.append(pl.BlockSpec((1, tt, 512), tok))
    ins += [mu, mum, w_wa]
    specs += [pl.BlockSpec((1, 1536), full), pl.BlockSpec((1, 512), full), pl.BlockSpec((128, 1024), full)]
    if has_vres:
        ins.append(w_v)
        specs.append(pl.BlockSpec((128, 512), full))
    ins += [w_g, headsum, wbias, abias]
    specs += [pl.BlockSpec((128, 512), full), pl.BlockSpec((512, 512), full), vec, vec]
    if has_vres:
        ins.append(vbias)
        specs.append(vec)
    ins += [kkw, kaw, rkw]
    specs += [vec, vec, vec]
    n_out = 8 if has_vres else 9
    dts = [BF16, F32, BF16, BF16, BF16, BF16, BF16, BF16] + ([] if has_vres else [BF16])
    return pl.pallas_call(
        functools.partial(_rprep_kernel, tt=tt, has_vres=has_vres),
        grid=(bsz, s // tt),
        in_specs=specs,
        out_specs=[pl.BlockSpec((1, tt, 512), tok)] * n_out,
        out_shape=[jax.ShapeDtypeStruct((bsz, s, 512), dt) for dt in dts],
        scratch_shapes=[pltpu.VMEM((8, 1536), F32), pltpu.VMEM((8, 512), F32)],
        compiler_params=_cp("parallel", "arbitrary"),
        name="rwkv_prep",
    )(*ins)


def _rchunk_kernel(r_ref, lw_ref, k_ref, v_ref, kk_ref, b_ref, m_o, n_o, rq_o, y0_o, *, nck):
    L = R_CHUNK
    ri = lax.broadcasted_iota(jnp.int32, (L, L), 0)
    ci = lax.broadcasted_iota(jnp.int32, (L, L), 1)
    lower = (ci <= ri).astype(BF16)
    strict = ci < ri
    incl = ci <= ri
    eye = (ci == ri).astype(F32)

    kk_t, r_t, b_h, k_h, b_w, k_w, w_last, vf = [], [], [], [], [], [], [], []
    for c in range(nck):
        rs = slice(c * L, (c + 1) * L)
        lw = lw_ref[0, rs, :]
        l1, l2 = _split(lw)
        cl = _dot(lower, l1) + _dot(lower, l2)
        cl_last = cl[L - 1:L, :]
        w_inv = jnp.exp(-cl)
        w_rem = jnp.exp(cl_last - cl)
        kf = k_ref[0, rs, :].astype(F32)
        bf = b_ref[0, rs, :].astype(F32)
        kk_t.append((kk_ref[0, rs, :].astype(F32) * jnp.exp(cl - lw)).astype(BF16))
        r_t.append(r_ref[0, rs, :].astype(F32) * jnp.exp(cl))
        b_h.append((bf * w_inv).astype(BF16))
        k_h.append((kf * w_inv).astype(BF16))
        b_w.append((bf * w_rem).astype(BF16))
        k_w.append((kf * w_rem).astype(BF16))
        w_last.append(jnp.exp(cl_last))
        vf.append(v_ref[0, rs, :])

    J = [(c, slice(h * R_DH, (h + 1) * R_DH)) for c in range(nck) for h in range(R_HEADS)]
    N = range(len(J))
    a = [_dot_nt(jnp.concatenate([kk_t[c][:, sl], r_t[c][:, sl].astype(BF16)], axis=0),
                 jnp.concatenate([b_h[c][:, sl], k_h[c][:, sl]], axis=0)) for c, sl in J]
    a_ak = [jnp.where(strict, a[j][:L, L:], 0.0).astype(BF16) for j in N]
    a_rb = [jnp.where(incl, a[j][L:, :L], 0.0).astype(BF16) for j in N]
    a_rk = [jnp.where(incl, a[j][L:, L:], 0.0).astype(BF16) for j in N]
    x = [jnp.where(strict, -a[j][:L, :L], 0.0) for j in N]
    t = [eye + x[j] for j in N]
    for _ in range(5):
        x = [_dot16(x[j], x[j]) for j in N]
        t = [t[j] + _dot16(t[j], x[j]) for j in N]
    vh = [vf[c][:, sl] for c, sl in J]
    akv = [_dot(a_ak[j], vh[j]) for j in N]
    rkv = [_dot(a_rk[j], vh[j]) for j in N]
    pq16 = [_dot16(t[j], jnp.concatenate([kk_t[c][:, sl].astype(F32), akv[j]], axis=1)).astype(BF16)
            for j, (c, sl) in enumerate(J)]
    rq_y0 = [jnp.concatenate([r_t[c][:, sl], rkv[j]], axis=1) - _dot(a_rb[j], pq16[j]) for j, (c, sl) in enumerate(J)]
    pqb = [_dot_tn(pq16[j], b_w[c][:, sl]) for j, (c, sl) in enumerate(J)]
    vk = [_dot_tn(vh[j], k_w[c][:, sl]) for j, (c, sl) in enumerate(J)]
    for j, (c, sl) in enumerate(J):
        rs = slice(c * L, (c + 1) * L)
        rq_o[0, rs, sl] = rq_y0[j][:, :R_DH]
        y0_o[0, rs, sl] = rq_y0[j][:, R_DH:]
        h = j % R_HEADS
        m_o[0, c, h] = eye * w_last[c][:, sl] - pqb[j][:R_DH]
        n_o[0, c, h] = vk[j] - pqb[j][R_DH:]


def _rwkv_chunks(r, lw, k, v, kk, b, nck):
    bsz, s, w = r.shape
    L = R_CHUNK
    nc = s // L
    tok = lambda bi, c: (bi, c, 0)
    st = lambda bi, c: (bi, c, 0, 0, 0)
    sshape = (bsz, nc, R_HEADS, R_DH, R_DH)
    sblock = (1, nck, R_HEADS, R_DH, R_DH)
    return pl.pallas_call(
        functools.partial(_rchunk_kernel, nck=nck),
        grid=(bsz, nc // nck),
        in_specs=[pl.BlockSpec((1, nck * L, w), tok)] * 6,
        out_specs=[pl.BlockSpec(sblock, st), pl.BlockSpec(sblock, st),
                   pl.BlockSpec((1, nck * L, w), tok), pl.BlockSpec((1, nck * L, w), tok)],
        out_shape=[jax.ShapeDtypeStruct(sshape, F32), jax.ShapeDtypeStruct(sshape, F32),
                   jax.ShapeDtypeStruct((bsz, s, w), F32), jax.ShapeDtypeStruct((bsz, s, w), F32)],
        compiler_params=_cp("parallel", "parallel"),
        name="rwkv_chunks",
    )(r, lw, k, v, kk, b)


def _rstate_kernel(m_ref, n_ref, s_ref, s_sc, *, nck):
    H = range(R_HEADS)

    @pl.when(pl.program_id(1) == 0)
    def _():
        s_sc[...] = jnp.zeros(s_sc.shape, F32)

    for c in range(nck):
        st = [s_sc[h] for h in H]
        sp = [_split(st[h]) for h in H]
        mp = [_split(m_ref[0, c, h]) for h in H]
        snew = [_dot(sp[h][0], mp[h][0]) + (_dot(sp[h][1], mp[h][0]) + _dot(sp[h][0], mp[h][1])) for h in H]
        for h in H:
            s_ref[0, c, h] = st[h]
            s_sc[h] = snew[h] + n_ref[0, c, h]


def _rout_kernel(s_ref, rq_ref, y0_ref, bv_ref, g_ref, lng_ref, lnb_ref, o_ref, *, nck):
    L = R_CHUNK
    J = [(c, h) for c in range(nck) for h in range(R_HEADS)]
    rsl = lambda c: slice(c * L, (c + 1) * L)
    hsl = lambda h: slice(h * R_DH, (h + 1) * R_DH)
    sp = [_split(s_ref[0, c, h]) for c, h in J]
    rp = [_split(rq_ref[0, rsl(c), hsl(h)]) for c, h in J]
    y = [y0_ref[0, rsl(c), hsl(h)] + _dot_nt(rp[j][0], sp[j][0])
         + (_dot_nt(rp[j][1], sp[j][0]) + _dot_nt(rp[j][0], sp[j][1])) for j, (c, h) in enumerate(J)]
    for j, (c, h) in enumerate(J):
        rs, sl = rsl(c), hsl(h)
        mu = jnp.mean(y[j], axis=1, keepdims=True)
        yc = y[j] - mu
        var = jnp.mean(yc * yc, axis=1, keepdims=True)
        yn = yc * lax.rsqrt(var + R_GN_EPS) * lng_ref[:, sl] + lnb_ref[:, sl]
        o_ref[0, rs, sl] = ((yn + bv_ref[0, rs, sl].astype(F32)) * g_ref[0, rs, sl].astype(F32)).astype(o_ref.dtype)


def _rwkv_scan(m, n, rq, y0, bv, g, lng, lnb, nck_state, nck_out):
    bsz, s, w = rq.shape
    L = R_CHUNK
    nc = s // L
    tok = lambda bi, c: (bi, c, 0)
    st = lambda bi, c: (bi, c, 0, 0, 0)
    full = lambda bi, c: (0, 0)
    sblock = lambda k: (1, k, R_HEADS, R_DH, R_DH)
    states = pl.pallas_call(
        functools.partial(_rstate_kernel, nck=nck_state),
        grid=(bsz, nc // nck_state),
        in_specs=[pl.BlockSpec(sblock(nck_state), st), pl.BlockSpec(sblock(nck_state), st)],
        out_specs=pl.BlockSpec(sblock(nck_state), st),
        out_shape=jax.ShapeDtypeStruct(m.shape, F32),
        scratch_shapes=[pltpu.VMEM((R_HEADS, R_DH, R_DH), F32)],
        compiler_params=_cp("parallel", "arbitrary"),
        name="rwkv_state",
    )(m, n)
    k = nck_out
    return pl.pallas_call(
        functools.partial(_rout_kernel, nck=k),
        grid=(bsz, nc // k),
        in_specs=[pl.BlockSpec(sblock(k), st),
                  pl.BlockSpec((1, k * L, w), tok), pl.BlockSpec((1, k * L, w), tok),
                  pl.BlockSpec((1, k * L, w), tok), pl.BlockSpec((1, k * L, w), tok),
                  pl.BlockSpec((1, w), full), pl.BlockSpec((1, w), full)],
        out_specs=pl.BlockSpec((1, k * L, w), tok),
        out_shape=jax.ShapeDtypeStruct((bsz, s, w), BF16),
        compiler_params=_cp("parallel", "parallel"),
        name="rwkv_out",
    )(states, rq, y0, bv, g, lng, lnb)


def _merge_kernel(ha_ref, hb_ref, y_ref, gate_ref, x_ref, mup_ref, fup_ref, rup_ref, wo_ref, g_ref, b_ref,
                  o32_ref, o16_ref):
    d = D_MODEL
    merged = (_sigmoid(gate_ref[:, 0:d].astype(F32)) * _dot(ha_ref[...], mup_ref[...])
              + _sigmoid(gate_ref[:, d:2 * d].astype(F32)) * _dot(hb_ref[...], fup_ref[...])
              + _sigmoid(gate_ref[:, 2 * d:3 * d].astype(F32)) * _dot(y_ref[...], rup_ref[...]))
    mix = _dot(merged.astype(BF16), wo_ref[...])
    out = _ln(ALPHA * x_ref[...] + mix, g_ref[...], b_ref[...])
    o32_ref[...] = out
    o16_ref[...] = out.astype(BF16)


def _merge(ha, hb, y, p16, x32, mup, fup, rup, wo, g, b, tm):
    t, d = x32.shape
    tm = min(tm, t)
    full = lambda i: (0, 0)
    tok = lambda i: (i, 0)
    return pl.pallas_call(
        _merge_kernel,
        grid=(t // tm,),
        in_specs=[pl.BlockSpec((tm, 512), tok), pl.BlockSpec((tm, 512), tok), pl.BlockSpec((tm, 512), tok),
                  pl.BlockSpec((tm, 3 * d), lambda i: (i, C_GATE // (3 * d))),
                  pl.BlockSpec((tm, d), tok),
                  pl.BlockSpec((512, d), full), pl.BlockSpec((512, d), full), pl.BlockSpec((512, d), full),
                  pl.BlockSpec((d, d), full), pl.BlockSpec((1, d), full), pl.BlockSpec((1, d), full)],
        out_specs=[pl.BlockSpec((tm, d), tok), pl.BlockSpec((tm, d), tok)],
        out_shape=[jax.ShapeDtypeStruct((t, d), F32), jax.ShapeDtypeStruct((t, d), BF16)],
        compiler_params=_cp("parallel"),
        name="merge_ln1",
    )(ha, hb, y, p16, x32, mup, fup, rup, wo, g, b)


def _xattn_kernel(x16_ref, x32_ref, k_ref, v_ref, wq_ref, wo_ref, g_ref, b_ref, o32_ref, o16_ref):
    q = _dot(x16_ref[0], wq_ref[...]).astype(BF16)
    outs = []
    for h in range(X_HEADS):
        sl = slice(h * X_DH, (h + 1) * X_DH)
        lg = _dot_nt(q[:, sl], k_ref[0, :, sl]) * (X_DH ** -0.5)
        mx = jnp.max(lg, axis=1, keepdims=True)
        e = jnp.exp(lg - mx)
        p = e / jnp.sum(e, axis=1, keepdims=True)
        outs.append(_dot(p.astype(BF16), v_ref[0, :, sl]))
    o = jnp.concatenate(outs, axis=1).astype(BF16)
    out = _ln(ALPHA * x32_ref[0] + _dot(o, wo_ref[...]), g_ref[...], b_ref[...])
    o32_ref[0] = out
    o16_ref[0] = out.astype(BF16)


def _xattn(x16, x32, kk, vv, wq, wo, g, b, tm):
    bsz, s, d = x32.shape
    tm = min(tm, s)
    m = kk.shape[1]
    full = lambda bi, i: (0, 0)
    tok = lambda bi, i: (bi, i, 0)
    mem = lambda bi, i: (bi, 0, 0)
    return pl.pallas_call(
        _xattn_kernel,
        grid=(bsz, s // tm),
        in_specs=[pl.BlockSpec((1, tm, d), tok), pl.BlockSpec((1, tm, d), tok),
                  pl.BlockSpec((1, m, d), mem), pl.BlockSpec((1, m, d), mem),
                  pl.BlockSpec((d, d), full), pl.BlockSpec((d, d), full),
                  pl.BlockSpec((1, d), full), pl.BlockSpec((1, d), full)],
        out_specs=[pl.BlockSpec((1, tm, d), tok), pl.BlockSpec((1, tm, d), tok)],
        out_shape=[jax.ShapeDtypeStruct((bsz, s, d), F32), jax.ShapeDtypeStruct((bsz, s, d), BF16)],
        compiler_params=_cp("parallel", "parallel"),
        name="xattn_ln2",
    )(x16, x32, kk, vv, wq, wo, g, b)


def _ffn_kernel(x16_ref, x32_ref, wg_ref, wu_ref, wd_ref, g_ref, b_ref, o32_ref, o16_ref, acc_sc):
    j = pl.program_id(1)

    @pl.when(j == 0)
    def _():
        acc_sc[...] = jnp.zeros(acc_sc.shape, F32)

    x = x16_ref[...]
    act = _silu(_dot(x, wg_ref[...])) * _dot(x, wu_ref[...])
    acc_sc[...] += _dot(act.astype(BF16), wd_ref[...])

    @pl.when(j == pl.num_programs(1) - 1)
    def _():
        out = _ln(ALPHA * x32_ref[...] + acc_sc[...], g_ref[...], b_ref[...])
        o32_ref[...] = out
        o16_ref[...] = out.astype(BF16)


def _ffn(x16, x32, wgu, wd, g, b, tm, bf):
    t, d = x32.shape
    tm = min(tm, t)
    dff = wd.shape[0]
    nf = dff // bf
    tok = lambda i, j: (i, 0)
    full = lambda i, j: (0, 0)
    return pl.pallas_call(
        _ffn_kernel,
        grid=(t // tm, nf),
        in_specs=[pl.BlockSpec((tm, d), tok), pl.BlockSpec((tm, d), tok),
                  pl.BlockSpec((d, bf), lambda i, j: (0, j)),
                  pl.BlockSpec((d, bf), lambda i, j: (0, j + nf)),
                  pl.BlockSpec((bf, d), lambda i, j: (j, 0)),
                  pl.BlockSpec((1, d), full), pl.BlockSpec((1, d), full)],
        out_specs=[pl.BlockSpec((tm, d), tok), pl.BlockSpec((tm, d), tok)],
        out_shape=[jax.ShapeDtypeStruct((t, d), F32), jax.ShapeDtypeStruct((t, d), BF16)],
        scratch_shapes=[pltpu.VMEM((tm, d), F32)],
        compiler_params=_cp("parallel", "arbitrary"),
        name="ffn_ln3",
    )(x16, x32, wgu, wgu, wd, g, b)


RT_E1, RT_E2, RT_W1, RT_W2, RT_R1, RT_R2 = 0, 1, 2, 3, 4, 5


def _route_kernel(x_ref, rw_ref, rb_ref, info_ref, cnt_ref, carry_sc, *, tr):
    @pl.when(pl.program_id(0) == 0)
    def _():
        carry_sc[...] = jnp.zeros(carry_sc.shape, F32)

    lane = lax.broadcasted_iota(jnp.int32, (tr, 128), 1).astype(F32)
    logits = _dot_x3(x_ref[...], rw_ref[...]) + rb_ref[...]
    logits = jnp.where(lane < N_EXPERTS, logits, -jnp.inf)
    m1 = jnp.max(logits, axis=1, keepdims=True)
    i1 = jnp.min(jnp.where(logits == m1, lane, 128.0), axis=1, keepdims=True)
    rest = jnp.where(lane == i1, -jnp.inf, logits)
    m2 = jnp.max(rest, axis=1, keepdims=True)
    i2 = jnp.min(jnp.where(rest == m2, lane, 128.0), axis=1, keepdims=True)
    e2 = jnp.exp(m2 - m1)
    w1 = 1.0 / (1.0 + e2)
    w2 = e2 / (1.0 + e2)
    oh1 = lane == i1
    oh2 = lane == i2
    oh = jnp.logical_or(oh1, oh2).astype(F32)
    r = lax.broadcasted_iota(jnp.int32, (tr, tr), 0)
    c = lax.broadcasted_iota(jnp.int32, (tr, tr), 1)
    before = (c < r).astype(BF16)
    carry = carry_sc[0:1, :]
    rank = _dot(before, oh.astype(BF16)) + carry
    r1 = jnp.sum(jnp.where(oh1, rank, 0.0), axis=1, keepdims=True)
    r2 = jnp.sum(jnp.where(oh2, rank, 0.0), axis=1, keepdims=True)
    total = carry + jnp.sum(oh, axis=0, keepdims=True)
    carry_sc[...] = jnp.broadcast_to(total, carry_sc.shape)
    cnt_ref[...] = jnp.broadcast_to(total, cnt_ref.shape)
    rec = jnp.zeros((tr, 128), F32)
    for ln, val in ((RT_E1, i1), (RT_E2, i2), (RT_W1, w1), (RT_W2, w2), (RT_R1, r1), (RT_R2, r2)):
        rec = jnp.where(lane == float(ln), val, rec)
    info_ref[...] = rec


def _route(x32, rw, rb, tr):
    t, d = x32.shape
    tr = min(tr, t)
    full = lambda i: (0, 0)
    return pl.pallas_call(
        functools.partial(_route_kernel, tr=tr),
        grid=(t // tr,),
        in_specs=[pl.BlockSpec((tr, d), lambda i: (i, 0)), pl.BlockSpec((d, 128), full), pl.BlockSpec((1, 128), full)],
        out_specs=[pl.BlockSpec((tr, 128), lambda i: (i, 0)), pl.BlockSpec((8, 128), full)],
        out_shape=[jax.ShapeDtypeStruct((t, 128), F32), jax.ShapeDtypeStruct((8, 128), F32)],
        scratch_shapes=[pltpu.VMEM((8, 128), F32)],
        compiler_params=_cp("arbitrary"),
        name="moe_route",
    )(x32, rw, rb)


def _experts_kernel(te_ref, nu_ref, src_ref, srcn_ref, dst_ref, x_hbm, wg_ref, wu_ref, wd_ref, yinit_hbm, yg_hbm,
                    xbuf, x16_sc, acc_sc, ybuf, gsem, ssem, *, tm):
    del te_ref, yinit_hbm
    i = pl.program_id(0)
    j = pl.program_id(1)
    nf = pl.num_programs(1)
    nu = nu_ref[0]
    slot = i % 2

    def gather(idx_ref, s):
        return lambda r: pltpu.make_async_copy(x_hbm.at[pl.ds(idx_ref[0, 0, r], 1)], xbuf.at[s, pl.ds(r, 1)], gsem.at[s])

    def scatter(s):
        return lambda r: pltpu.make_async_copy(ybuf.at[s, pl.ds(r, 1)], yg_hbm.at[pl.ds(dst_ref[0, 0, r], 1)], ssem.at[s])

    def start_all(copy):
        lax.fori_loop(0, tm, lambda r, c: (copy(r).start(), c)[1], 0, unroll=8)

    def wait_gather(s):
        pltpu.make_async_copy(x_hbm.at[pl.ds(0, tm)], xbuf.at[s], gsem.at[s]).wait()

    def wait_scatter(s):
        pltpu.make_async_copy(ybuf.at[s], yg_hbm.at[pl.ds(0, tm)], ssem.at[s]).wait()

    @pl.when(i < nu)
    def _():
        @pl.when(j == 0)
        def _():
            @pl.when(i == 0)
            def _():
                start_all(gather(src_ref, slot))
            wait_gather(slot)
            x16_sc[...] = xbuf[slot].astype(BF16)
            acc_sc[...] = jnp.zeros(acc_sc.shape, F32)

        @pl.when(jnp.logical_and(j == 1, i + 1 < nu))
        def _():
            start_all(gather(srcn_ref, 1 - slot))

        x = x16_sc[...]
        act = _silu(_dot(x, wg_ref[0])) * _dot(x, wu_ref[0])
        acc_sc[...] += _dot(act.astype(BF16), wd_ref[0])

        @pl.when(j == nf - 1)
        def _():
            @pl.when(i > 0)
            def _():
                wait_scatter(1 - slot)
            ybuf[slot] = acc_sc[...]
            start_all(scatter(slot))

            @pl.when(i == nu - 1)
            def _():
                wait_scatter(slot)


def _experts(tile_expert, n_used, src_row, dst_row, x32, wgu, wd, tm, bf):
    rtot = src_row.shape[0]
    d = x32.shape[1]
    ne, dff, _ = wd.shape
    nf = dff // bf
    nt = rtot // tm
    assert nf >= 2
    ex = lambda i, te, nu: te[jnp.minimum(i, nu[0] - 1)]
    idx = lambda f: pl.BlockSpec((1, 1, tm), lambda i, j, te, nu: (f(i, nu), 0, 0), memory_space=pltpu.SMEM)
    cur = lambda i, nu: jnp.minimum(i, nu[0] - 1)
    nxt = lambda i, nu: jnp.minimum(i + 1, nu[0] - 1)
    src3 = src_row.reshape(nt, 1, tm)
    return pl.pallas_call(
        functools.partial(_experts_kernel, tm=tm),
        grid_spec=pltpu.PrefetchScalarGridSpec(
            num_scalar_prefetch=2,
            grid=(nt, nf),
            in_specs=[idx(cur), idx(nxt), idx(cur),
                      pl.BlockSpec(memory_space=pl.ANY),
                      pl.BlockSpec((1, d, bf), lambda i, j, te, nu: (ex(i, te, nu), 0, j)),
                      pl.BlockSpec((1, d, bf), lambda i, j, te, nu: (ex(i, te, nu), 0, j + nf)),
                      pl.BlockSpec((1, bf, d), lambda i, j, te, nu: (ex(i, te, nu), j, 0)),
                      pl.BlockSpec(memory_space=pl.ANY)],
            out_specs=pl.BlockSpec(memory_space=pl.ANY),
            scratch_shapes=[pltpu.VMEM((2, tm, d), F32), pltpu.VMEM((tm, d), BF16), pltpu.VMEM((tm, d), F32),
                            pltpu.VMEM((2, tm, d), F32), pltpu.SemaphoreType.DMA((2,)), pltpu.SemaphoreType.DMA((2,))]),
        out_shape=jax.ShapeDtypeStruct((rtot, d), F32),
        input_output_aliases={9: 0},
        compiler_params=_cp("arbitrary", "arbitrary"),
        name="moe_experts",
    )(tile_expert, n_used, src3, src3, dst_row.reshape(nt, 1, tm), x32, wgu, wgu, wd, jnp.zeros((rtot, d), F32))


def _combine_kernel(x_ref, y1_ref, y2_ref, info_ref, g_ref, b_ref, o_ref):
    info = info_ref[...]
    w1 = info[:, RT_W1:RT_W1 + 1]
    w2 = info[:, RT_W2:RT_W2 + 1]
    o_ref[...] = _ln(ALPHA * x_ref[...] + (w1 * y1_ref[...] + w2 * y2_ref[...]), g_ref[...], b_ref[...])


def _combine(x32, yg, info, g, b, tm):
    t, d = x32.shape
    tm = min(tm, t)
    nt = t // tm
    tok = lambda i: (i, 0)
    full = lambda i: (0, 0)
    return pl.pallas_call(
        _combine_kernel,
        grid=(nt,),
        in_specs=[pl.BlockSpec((tm, d), tok), pl.BlockSpec((tm, d), tok), pl.BlockSpec((tm, d), lambda i: (i + nt, 0)),
                  pl.BlockSpec((tm, 128), tok), pl.BlockSpec((1, d), full), pl.BlockSpec((1, d), full)],
        out_specs=pl.BlockSpec((tm, d), tok),
        out_shape=jax.ShapeDtypeStruct((t, d), F32),
        compiler_params=_cp("parallel"),
        name="moe_combine_ln3",
    )(x32, yg, yg, info, g, b)


def _moe(x32, rw, rb, wgu, wd, g, b, tm, bf):
    t, d = x32.shape
    tm = min(tm, t)
    info, cnt = _route(x32, rw, rb, 512)
    i32 = jnp.int32
    e1, e2 = info[:, RT_E1].astype(i32), info[:, RT_E2].astype(i32)
    r1, r2 = info[:, RT_R1].astype(i32), info[:, RT_R2].astype(i32)
    tiles_e = (cnt[0, :N_EXPERTS].astype(i32) + tm - 1) // tm
    tile_end = jnp.cumsum(tiles_e)
    start = (tile_end - tiles_e) * tm
    nt = (2 * t) // tm + N_EXPERTS
    n_used = tile_end[N_EXPERTS - 1:]
    tile_expert = jnp.minimum(jnp.searchsorted(tile_end, jnp.arange(nt, dtype=i32), side='right'),
                              N_EXPERTS - 1).astype(i32)
    pos = jnp.concatenate([start[e1] + r1, start[e2] + r2])
    rows = nt * tm
    slot_row = jnp.full((rows,), -1, i32).at[pos].set(jnp.arange(2 * t, dtype=i32))
    pad = slot_row < 0
    src_row = jnp.where(pad, 0, slot_row % t)
    dst_row = jnp.where(pad, 2 * t + jnp.cumsum(pad.astype(i32)) - 1, slot_row)
    yg = _experts(tile_expert, n_used, src_row, dst_row, x32, wgu, wd, tm, bf)
    return _combine(x32, yg, info, g, b, tm)


def _src_layout(with_vres):
    cols = [('m_qk', 512), ('m_v', 512), ('m_o', 512), ('m_i', 4), ('m_f', 4),
            ('f_q', 512), ('f_k', 512), ('f_v', 512), ('f_f', 8), ('gate', 3072),
            ('r_r', 512), ('r_k', 512), ('r_v', 512), ('r_w', 64), ('r_a', 64), ('r_g', 128)]
    if with_vres:
        cols.append(('r_vres', 32))
    layout, start = {}, 0
    for name, width in cols:
        layout[name] = (start, start + width)
        start += width
    return layout


def _relayout_cols(w, with_vres):
    lay = _src_layout(with_vres)
    lead = w.shape[:-1]
    z = lambda n: jnp.zeros(lead + (n,), w.dtype)
    c = lambda name: w[..., lay[name][0]:lay[name][1]]

    def slots(x, scale):
        x = x.reshape(lead + (F_HEADS, F_DH)) * scale
        return jnp.concatenate([x, jnp.zeros_like(x)], axis=-1).reshape(lead + (F_HEADS * F_PAD,))

    parts = [c('gate'), c('r_r'), c('r_k'), c('r_v'),
             c('r_w'), c('r_a'), c('r_g'),
             c('m_i'), c('m_f'), c('f_f'), z(16), c('r_vres') if with_vres else z(32), z(64), z(128),
             slots(c('f_q'), F_DH ** -0.5 * LOG2E), slots(c('f_k'), 1.0),
             c('m_qk'), c('m_v'), c('m_o'), c('f_v')]
    return jnp.concatenate(parts, axis=-1)


def _row(v):
    return v.reshape(1, -1).astype(F32)


def _branches(x16, bsz, s, prm, v_first):
    (w_in, b_in, m_conv, m_norm, m_up, f_up, r_mu, r_wbias, r_wB, r_abias, r_aB, r_vbias, r_vB,
     r_gB, r_kk, r_ka, r_rk, r_ln_g, r_ln_b, r_up, w_out, ln1_g, ln1_b) = prm
    with_vres = r_vbias is not None
    w_all = _relayout_cols(w_in, with_vres)
    b_all = _relayout_cols(b_in, with_vres).reshape(1, NP)
    p16 = _matmul_bias(x16, w_all.astype(BF16), b_all, BF16, 1024, 1024, "in_proj")
    gates = _matmul_bias(x16, w_all[:, C_SMALL:C_SMALL + 128].astype(BF16), b_all[:, C_SMALL:C_SMALL + 128],
                         F32, 2048, 128, "in_proj_gates")
    p16 = p16.reshape(bsz, s, NP)
    gates = gates.reshape(bsz, s, 128)
    gates_t = jnp.transpose(gates, (0, 2, 1))

    h_a = _mlstm(p16, gates, gates_t, m_conv.astype(F32), _row(m_norm), 256)
    q_aug, k_aug, v_aug = _fox_prep(p16, gates_t, 512)
    h_b = _fox_attention(q_aug, k_aug, v_aug, 512)
    lay = _src_layout(with_vres)
    r0 = lay['r_r'][0]
    mu = r_mu[:1536].reshape(1, 1536)
    mum = jnp.concatenate([r_mu[1536:1792], jnp.zeros((32,), F32),
                           r_mu[lay['r_vres'][0] - r0:] if with_vres else jnp.zeros((32,), F32),
                           jnp.zeros((192,), F32)]).reshape(1, 512)
    z = jnp.zeros((64, 512), F32)
    w_wa = jnp.concatenate([jnp.concatenate([r_wB, z], axis=1), jnp.concatenate([z, r_aB], axis=1)], axis=0).astype(BF16)
    w_v = None
    if with_vres:
        w_v = jnp.concatenate([jnp.zeros((32, 512), F32), r_vB, jnp.zeros((64, 512), F32)], axis=0).astype(BF16)
    hid = jnp.arange(512) // R_DH
    headsum = (hid[:, None] == hid[None, :]).astype(BF16)
    outs = _rwkv_prep(p16, v_first, mu, mum, w_wa, w_v, r_gB.astype(BF16), headsum, _row(r_wbias), _row(r_abias),
                      _row(r_vbias) if with_vres else None, _row(r_kk), _row(r_ka), _row(r_rk), 256)
    r, lw, k2, v, kk, b, g, bv = outs[:8]
    v_own = v_first if with_vres else outs[8]
    m_c, n_c, rq, y0 = _rwkv_chunks(r, lw, k2, v, kk, b, 2)
    y = _rwkv_scan(m_c, n_c, rq, y0, bv, g, _row(r_ln_g), _row(r_ln_b), 8, 2)
    return p16, h_a, h_b, y, v_own


def _token_mixer(x16, x32, bsz, s, prm, v_first):
    m_up, f_up, r_up, w_out, ln1_g, ln1_b = prm[4], prm[5], prm[19], prm[20], prm[21], prm[22]
    t = bsz * s
    p16, h_a, h_b, y, v_own = _branches(x16, bsz, s, prm, v_first)
    x1_32, x1_16 = _merge(h_a.reshape(t, 512), h_b.reshape(t, 512), y.reshape(t, 512), p16.reshape(t, NP), x32,
                          m_up.astype(BF16), f_up.astype(BF16), r_up.astype(BF16), w_out.astype(BF16),
                          _row(ln1_g), _row(ln1_b), 512)
    return x1_32, x1_16, v_own


def kernel(x, mem, mem_ln_g, mem_ln_b, w_in_0, b_in_0, m_conv_0, m_norm_0, m_up_0, f_up_0, r_mu_0, r_wbias_0, r_wB_0, r_abias_0, r_aB_0, r_gB_0, r_kk_0, r_ka_0, r_rk_0, r_ln_g_0, r_ln_b_0, r_up_0, w_out_0, ln1_g_0, ln1_b_0, x_wq_0, x_wkv_0, x_wo_0, ln2_g_0, ln2_b_0, ff_wgu_0, ff_wd_0, ln3_g_0, ln3_b_0, w_in_1, b_in_1, m_conv_1, m_norm_1, m_up_1, f_up_1, r_mu_1, r_wbias_1, r_wB_1, r_abias_1, r_aB_1, r_vbias_1, r_vB_1, r_gB_1, r_kk_1, r_ka_1, r_rk_1, r_ln_g_1, r_ln_b_1, r_up_1, w_out_1, ln1_g_1, ln1_b_1, x_wq_1, x_wkv_1, x_wo_1, ln2_g_1, ln2_b_1, ex_router_1, ex_router_b_1, ex_wgu_1, ex_wd_1, ln3_g_1, ln3_b_1):
    bsz, s, d = x.shape
    t = bsz * s
    x32 = x.reshape(t, d)
    x16 = x32.astype(BF16)

    mixers = (
        (w_in_0, b_in_0, m_conv_0, m_norm_0, m_up_0, f_up_0, r_mu_0, r_wbias_0, r_wB_0, r_abias_0, r_aB_0, None, None,
         r_gB_0, r_kk_0, r_ka_0, r_rk_0, r_ln_g_0, r_ln_b_0, r_up_0, w_out_0, ln1_g_0, ln1_b_0),
        (w_in_1, b_in_1, m_conv_1, m_norm_1, m_up_1, f_up_1, r_mu_1, r_wbias_1, r_wB_1, r_abias_1, r_aB_1, r_vbias_1, r_vB_1,
         r_gB_1, r_kk_1, r_ka_1, r_rk_1, r_ln_g_1, r_ln_b_1, r_up_1, w_out_1, ln1_g_1, ln1_b_1),
    )
    xattn = ((x_wq_0, x_wo_0, ln2_g_0, ln2_b_0), (x_wq_1, x_wo_1, ln2_g_1, ln2_b_1))
    kv0, kv1 = _mem_kv(mem, _row(mem_ln_g), _row(mem_ln_b), x_wkv_0.astype(BF16), x_wkv_1.astype(BF16))
    kvs = (kv0, kv1)

    v_first = None
    for l in range(DEPTH):
        x32, x16, v_own = _token_mixer(x16, x32, bsz, s, mixers[l], v_first)
        if l == 0:
            v_first = v_own
        wq, wo, g2, b2 = xattn[l]
        kv = kvs[l]
        x32, x16 = _xattn(x16.reshape(bsz, s, d), x32.reshape(bsz, s, d), kv[:, :, :d], kv[:, :, d:],
                          wq.astype(BF16), wo.astype(BF16), _row(g2), _row(b2), 512)
        x32, x16 = x32.reshape(t, d), x16.reshape(t, d)
        if l % 2 == 0:
            x32, x16 = _ffn(x16, x32, ff_wgu_0.astype(BF16), ff_wd_0.astype(BF16), _row(ln3_g_0), _row(ln3_b_0), 512, 1408)
        else:
            rw = jnp.concatenate([ex_router_1, jnp.zeros((d, 128 - N_EXPERTS), F32)], axis=1)
            rb = jnp.concatenate([ex_router_b_1, jnp.zeros((128 - N_EXPERTS,), F32)]).reshape(1, 128)
            x32 = _moe(x32, rw, rb, ex_wgu_1.astype(BF16), ex_wd_1.astype(BF16), _row(ln3_g_1), _row(ln3_b_1), 1024, 896)
    return x32.reshape(bsz, s, d)
```

```python
import functools

import jax
import jax.numpy as jnp
from jax import lax
from jax.experimental import pallas as pl
from jax.experimental.pallas import tpu as pltpu

F32 = jnp.float32
BF16 = jnp.bfloat16

D_MODEL = 1024
DEPTH = 2
M_HEADS, M_DV, M_DQK, M_CONV, M_CHUNK = 4, 128, 64, 4, 64
F_HEADS, F_DH, F_PAD = 8, 64, 128
R_HEADS, R_DH, R_CHUNK = 8, 64, 64
R_GN_EPS = 64e-5
X_HEADS, X_DH = 4, 256
N_EXPERTS = 8
ALPHA = (2.0 * DEPTH) ** 0.25
LN_EPS = 1e-5
HEAD_NORM_EPS = 1e-6
NEG = -1e30
LOG2E = 1.4426950408889634

C_GATE = 0
C_RKV = 3072
C_RMISC = 4608
C_SMALL = 4864
C_FQ = 5120
C_FK = 6144
C_MQK = 7168
C_MV = 7680
C_MO = 8192
C_FV = 8704
NP = 9216
VMEM_LIMIT = 56 * 1024 * 1024


def _cp(*sem):
    return pltpu.CompilerParams(dimension_semantics=sem, vmem_limit_bytes=VMEM_LIMIT)


def _dot(a, b):
    return jnp.dot(a, b, preferred_element_type=F32)


def _dot_nt(a, b):
    return lax.dot_general(a, b, (((1,), (1,)), ((), ())), preferred_element_type=F32)


def _dot_tn(a, b):
    return lax.dot_general(a, b, (((0,), (0,)), ((), ())), preferred_element_type=F32)


def _split(x):
    hi = x.astype(BF16)
    lo = (x - hi.astype(F32)).astype(BF16)
    return hi, lo


def _split3(x):
    h1 = x.astype(BF16)
    r1 = x - h1.astype(F32)
    h2 = r1.astype(BF16)
    h3 = (r1 - h2.astype(F32)).astype(BF16)
    return h1, h2, h3


def _dot_x2(a, b16):
    hi, lo = _split(a)
    return _dot(hi, b16) + _dot(lo, b16)


def _dot_x3(a, b):
    ah, al = _split(a)
    bh, bl = _split(b)
    return _dot(ah, bh) + (_dot(al, bh) + _dot(ah, bl))


def _dot16(a, b):
    return _dot(a.astype(BF16), b.astype(BF16))


def _sigmoid(x):
    return 1.0 / (1.0 + jnp.exp(-x))


def _log_sigmoid(x):
    return jnp.minimum(x, 0.0) - jnp.log(1.0 + jnp.exp(-jnp.abs(x)))


def _softplus(x):
    return jnp.maximum(x, 0.0) + jnp.log(1.0 + jnp.exp(-jnp.abs(x)))


def _silu(x):
    return x * _sigmoid(x)


def _ln(z, g, b, eps=LN_EPS):
    mu = jnp.mean(z, axis=-1, keepdims=True)
    zc = z - mu
    var = jnp.mean(zc * zc, axis=-1, keepdims=True)
    return zc * lax.rsqrt(var + eps) * g + b


def _mm_kernel(a_ref, w_ref, b_ref, o_ref):
    o_ref[...] = (_dot(a_ref[...], w_ref[...]) + b_ref[...]).astype(o_ref.dtype)


def _matmul_bias(a, w, b, out_dtype, tm, tn, name):
    m, k = a.shape
    n = w.shape[1]
    tm, tn = min(tm, m), min(tn, n)
    return pl.pallas_call(
        _mm_kernel,
        grid=(m // tm, n // tn),
        in_specs=[pl.BlockSpec((tm, k), lambda i, j: (i, 0)),
                  pl.BlockSpec((k, tn), lambda i, j: (0, j)),
                  pl.BlockSpec((1, tn), lambda i, j: (0, j))],
        out_specs=pl.BlockSpec((tm, tn), lambda i, j: (i, j)),
        out_shape=jax.ShapeDtypeStruct((m, n), out_dtype),
        compiler_params=_cp("parallel", "parallel"),
        name=name,
    )(a, w, b)


def _memkv_kernel(mem_ref, g_ref, b_ref, w0_ref, w1_ref, o0_ref, o1_ref):
    mn = _ln(mem_ref[0], g_ref[...], b_ref[...]).astype(BF16)
    o0_ref[0] = _dot(mn, w0_ref[...]).astype(BF16)
    o1_ref[0] = _dot(mn, w1_ref[...]).astype(BF16)


def _mem_kv(mem, g, b, wkv0, wkv1):
    bsz, m, d = mem.shape
    n = wkv0.shape[1]
    full = lambda i: (0, 0)
    return pl.pallas_call(
        _memkv_kernel,
        grid=(bsz,),
        in_specs=[pl.BlockSpec((1, m, d), lambda i: (i, 0, 0)),
                  pl.BlockSpec((1, d), full), pl.BlockSpec((1, d), full),
                  pl.BlockSpec((d, n), full), pl.BlockSpec((d, n), full)],
        out_specs=[pl.BlockSpec((1, m, n), lambda i: (i, 0, 0))] * 2,
        out_shape=[jax.ShapeDtypeStruct((bsz, m, n), BF16)] * 2,
        compiler_params=_cp("parallel"),
        name="mem_kv",
    )(mem, g, b, wkv0, wkv1)


def _foxprep_kernel(g_ref, q_ref, k_ref, v_ref, qo_ref, ko_ref, vo_ref, carry_sc, *, tt):
    blk = 128

    @pl.when(pl.program_id(1) == 0)
    def _():
        carry_sc[...] = jnp.zeros(carry_sc.shape, F32)

    r = lax.broadcasted_iota(jnp.int32, (blk, blk), 0)
    c = lax.broadcasted_iota(jnp.int32, (blk, blk), 1)
    upper = (r <= c).astype(BF16)
    carry = carry_sc[:, 0:1]
    parts = []
    for i in range(tt // blk):
        ls = _log_sigmoid(g_ref[0, :, i * blk:(i + 1) * blk])
        h1, h2, h3 = _split3(ls)
        cs = _dot(h1, upper) + (_dot(h2, upper) + _dot(h3, upper)) + carry
        carry = cs[:, blk - 1:blk]
        parts.append(cs)
    carry_sc[...] = jnp.broadcast_to(carry, carry_sc.shape)
    cum = jnp.concatenate(parts, axis=1) * LOG2E
    c1, c2, c3 = [p.astype(F32) for p in _split3(cum)]
    w = F_HEADS * F_PAD
    row = lax.broadcasted_iota(jnp.int32, (F_HEADS, w), 0)
    lane = lax.broadcasted_iota(jnp.int32, (F_HEADS, w), 1)
    sel = lambda j: (lane == row * F_PAD + F_DH + j).astype(F32)
    lane1 = lax.broadcasted_iota(jnp.int32, (1, w), 1) % F_PAD
    ones_q = jnp.logical_and(lane1 >= F_DH + 3, lane1 < F_DH + 6).astype(F32)
    ones_k = jnp.logical_and(lane1 >= F_DH, lane1 < F_DH + 3).astype(F32)
    aug_q = _dot_tn(c1, sel(0)) + _dot_tn(c2, sel(1)) + _dot_tn(c3, sel(2)) + ones_q
    aug_k = ones_k - (_dot_tn(c1, sel(3)) + _dot_tn(c2, sel(4)) + _dot_tn(c3, sel(5)))
    qo_ref[0] = (q_ref[0].astype(F32) + aug_q).astype(BF16)
    ko_ref[0] = (k_ref[0].astype(F32) + aug_k).astype(BF16)
    wv = F_HEADS * F_DH
    src = lax.broadcasted_iota(jnp.int32, (wv, w), 0)
    dst = lax.broadcasted_iota(jnp.int32, (wv, w), 1)
    place = (dst == (src // F_DH) * F_PAD + src % F_DH).astype(BF16)
    ones_v = (lane1 >= F_DH).astype(F32)
    vo_ref[0] = jnp.transpose(_dot(v_ref[0], place) + ones_v).astype(BF16)


def _fox_prep(p16, gates_t, tt):
    bsz, s, _ = p16.shape
    tt = min(tt, s)
    w = F_HEADS * F_PAD
    wv = F_HEADS * F_DH
    tok = lambda b, i: (b, i, 0)
    return pl.pallas_call(
        functools.partial(_foxprep_kernel, tt=tt),
        grid=(bsz, s // tt),
        in_specs=[pl.BlockSpec((1, 8, tt), lambda b, i: (b, 1, i)),
                  pl.BlockSpec((1, tt, w), lambda b, i: (b, i, C_FQ // w)),
                  pl.BlockSpec((1, tt, w), lambda b, i: (b, i, C_FK // w)),
                  pl.BlockSpec((1, tt, wv), lambda b, i: (b, i, C_FV // wv))],
        out_specs=[pl.BlockSpec((1, tt, w), tok), pl.BlockSpec((1, tt, w), tok),
                   pl.BlockSpec((1, w, tt), lambda b, i: (b, 0, i))],
        out_shape=[jax.ShapeDtypeStruct((bsz, s, w), BF16), jax.ShapeDtypeStruct((bsz, s, w), BF16),
                   jax.ShapeDtypeStruct((bsz, w, s), BF16)],
        scratch_shapes=[pltpu.VMEM((8, 128), F32)],
        compiler_params=_cp("parallel", "arbitrary"),
        name="fox_prep",
    )(gates_t, p16, p16, p16)


def _fox_kernel(qt_ref, kt_ref, q_ref, k_ref, v_ref, o_ref, m_sc, acc_sc, *, t):
    qi = qt_ref[pl.program_id(1)]
    ki = kt_ref[pl.program_id(1)]

    @pl.when(ki == 0)
    def _():
        m_sc[...] = jnp.full(m_sc.shape, NEG, F32)
        acc_sc[...] = jnp.zeros(acc_sc.shape, F32)

    def block(diag):
        if diag:
            key = lax.broadcasted_iota(jnp.int32, (t, t), 0)
            qry = lax.broadcasted_iota(jnp.int32, (t, t), 1)
            mask = key <= qry
        pss = [slice(h * F_PAD, (h + 1) * F_PAD) for h in range(F_HEADS)]
        scores = lambda h: _dot_nt(k_ref[0, :, pss[h]], q_ref[0, :, pss[h]])
        ahead = 2
        queue = [scores(h) for h in range(ahead)]
        for h in range(F_HEADS):
            s = queue.pop(0)
            if h + ahead < F_HEADS:
                queue.append(scores(h + ahead))
            if diag:
                s = jnp.where(mask, s, NEG)
            m_old = m_sc[h]
            m_new = jnp.maximum(m_old, jnp.max(s, axis=0, keepdims=True))
            p = jnp.exp2(s - m_new).astype(BF16)
            acc_sc[h] = jnp.exp2(m_old - m_new) * acc_sc[h] + _dot(v_ref[0, pss[h], :], p)
            m_sc[h] = m_new

    @pl.when(ki < qi)
    def _():
        block(False)

    @pl.when(ki == qi)
    def _():
        block(True)
        for h in range(F_HEADS):
            acc = acc_sc[h]
            out_t = acc[:F_DH, :] / acc[F_DH:F_DH + 1, :]
            o_ref[0, :, h * F_DH:(h + 1) * F_DH] = jnp.transpose(out_t).astype(o_ref.dtype)


def _fox_attention(q_aug, k_aug, v_aug_t, t):
    bsz, s, w = q_aug.shape
    t = min(t, s)
    n = s // t
    wv = F_HEADS * F_DH
    pairs = [(i, j) for i in range(n) for j in range(i + 1)]
    qt = jnp.asarray([p[0] for p in pairs], jnp.int32)
    kt = jnp.asarray([p[1] for p in pairs], jnp.int32)
    return pl.pallas_call(
        functools.partial(_fox_kernel, t=t),
        grid_spec=pltpu.PrefetchScalarGridSpec(
            num_scalar_prefetch=2,
            grid=(bsz, len(pairs)),
            in_specs=[pl.BlockSpec((1, t, w), lambda b, p, qt, kt: (b, qt[p], 0)),
                      pl.BlockSpec((1, t, w), lambda b, p, qt, kt: (b, kt[p], 0)),
                      pl.BlockSpec((1, w, t), lambda b, p, qt, kt: (b, 0, kt[p]))],
            out_specs=pl.BlockSpec((1, t, wv), lambda b, p, qt, kt: (b, qt[p], 0)),
            scratch_shapes=[pltpu.VMEM((F_HEADS, 1, t), F32), pltpu.VMEM((F_HEADS, F_PAD, t), F32)]),
        out_shape=jax.ShapeDtypeStruct((bsz, s, wv), BF16),
        compiler_params=_cp("parallel", "arbitrary"),
        name="fox_attention",
    )(qt, kt, q_aug, k_aug, v_aug_t)


def _mlstm_kernel(qk_ref, v_ref, o_ref, g_ref, gt_ref, cw_ref, nw_ref, out_ref,
                  prev_sc, q_sc, k_sc, c_sc, m_sc, *, lt):
    ti = pl.program_id(1)
    L = M_CHUNK
    H = range(M_HEADS)

    @pl.when(ti == 0)
    def _():
        prev_sc[...] = jnp.zeros(prev_sc.shape, F32)
        c_sc[...] = jnp.zeros(c_sc.shape, F32)
        m_sc[...] = jnp.zeros(m_sc.shape, F32)

    u = qk_ref[0].astype(F32)
    prev = prev_sc[...]
    rows = lax.broadcasted_iota(jnp.int32, u.shape, 0)
    acc = u * cw_ref[M_CONV - 1:M_CONV, :]
    for d in range(1, M_CONV):
        sh = jnp.where(rows < d, pltpu.roll(prev, d, 0), pltpu.roll(u, d, 0))
        acc = acc + sh * cw_ref[M_CONV - 1 - d:M_CONV - d, :]
    prev_sc[...] = u
    qk = _silu(acc)
    nq = M_HEADS * M_DQK
    q_sc[...] = qk[:, :nq].astype(BF16)
    k_sc[...] = qk[:, nq:] * (M_DQK ** -0.5)

    r = lax.broadcasted_iota(jnp.int32, (L, L), 0)
    c = lax.broadcasted_iota(jnp.int32, (L, L), 1)
    tri = c <= r
    lower = tri.astype(BF16)
    upper = (r <= c).astype(BF16)
    ones = jnp.ones((L, M_DV), BF16)

    for ch in range(lt // L):
        rs = slice(ch * L, (ch + 1) * L)
        gc = g_ref[0, rs, :]
        gr = gt_ref[0, :, rs]
        lc1, lc2 = _split(_log_sigmoid(gc))
        lr1, lr2 = _split(_log_sigmoid(gr))
        bcol = _dot(lower, lc1) + _dot(lower, lc2)
        brow = _dot(lr1, upper) + _dot(lr2, upper)
        hsl = [slice(h * M_DV, (h + 1) * M_DV) for h in H]
        qsl = [slice(h * M_DQK, (h + 1) * M_DQK) for h in H]
        bc = [bcol[:, 4 + h:5 + h] for h in H]
        m_prev = [m_sc[h:h + 1, 0:1] for h in H]
        qh = [q_sc[rs, qsl[h]] for h in H]
        kh = [k_sc[rs, qsl[h]] for h in H]
        vext = [jnp.concatenate([v_ref[0, rs, hsl[h]], ones], axis=1) for h in H]
        cext = [c_sc[h] for h in H]
        dm = [jnp.where(tri, bc[h] - brow[4 + h:5 + h, :] + gr[h:h + 1, :], -jnp.inf) for h in H]
        d_inter = [bc[h] + m_prev[h] for h in H]
        m_t = [jnp.maximum(d_inter[h], jnp.max(dm[h], axis=1, keepdims=True)) for h in H]
        qk_s = [_dot_nt(qh[h], kh[h].astype(BF16)) for h in H]
        qc = [_dot(qh[h], cext[h].astype(BF16)) for h in H]
        w_intra = [(qk_s[h] * jnp.exp(dm[h] - m_t[h])).astype(BF16) for h in H]
        ext = [jnp.exp(d_inter[h] - m_t[h]) * qc[h] + _dot(w_intra[h], vext[h]) for h in H]
        b_last = [bc[h][L - 1:L, :] for h in H]
        g = [b_last[h] - bc[h] + gc[:, h:h + 1] for h in H]
        m_new = [jnp.maximum(b_last[h] + m_prev[h], jnp.max(g[h], axis=0, keepdims=True)) for h in H]
        wk = [(jnp.exp(g[h] - m_new[h]) * kh[h]).astype(BF16) for h in H]
        upd = [_dot_tn(wk[h], vext[h]) for h in H]
        for h in H:
            c_sc[h] = jnp.exp(b_last[h] + m_prev[h] - m_new[h]) * cext[h] + upd[h]
            m_sc[h:h + 1, :] = jnp.broadcast_to(m_new[h], (1, m_sc.shape[1]))
        for h in H:
            hh = ext[h][:, :M_DV] / jnp.maximum(jnp.abs(ext[h][:, M_DV:M_DV + 1]), jnp.exp(-m_t[h]))
            mu = jnp.mean(hh, axis=1, keepdims=True)
            hc = hh - mu
            var = jnp.mean(hc * hc, axis=1, keepdims=True)
            hn = hc * lax.rsqrt(var + HEAD_NORM_EPS) * nw_ref[:, hsl[h]]
            out_ref[0, rs, hsl[h]] = (hn * _sigmoid(o_ref[0, rs, hsl[h]].astype(F32))).astype(out_ref.dtype)


def _mlstm(p16, gates, gates_t, conv_w, norm_w, lt):
    bsz, s, _ = p16.shape
    lt = min(lt, s)
    w = M_HEADS * M_DV
    full = lambda b, i: (0, 0)
    return pl.pallas_call(
        functools.partial(_mlstm_kernel, lt=lt),
        grid=(bsz, s // lt),
        in_specs=[pl.BlockSpec((1, lt, w), lambda b, i: (b, i, C_MQK // w)),
                  pl.BlockSpec((1, lt, w), lambda b, i: (b, i, C_MV // w)),
                  pl.BlockSpec((1, lt, w), lambda b, i: (b, i, C_MO // w)),
                  pl.BlockSpec((1, lt, 128), lambda b, i: (b, i, 0)),
                  pl.BlockSpec((1, 8, lt), lambda b, i: (b, 0, i)),
                  pl.BlockSpec((M_CONV, w), full),
                  pl.BlockSpec((1, w), full)],
        out_specs=pl.BlockSpec((1, lt, w), lambda b, i: (b, i, 0)),
        out_shape=jax.ShapeDtypeStruct((bsz, s, w), BF16),
        scratch_shapes=[pltpu.VMEM((lt, w), F32),
                        pltpu.VMEM((lt, M_HEADS * M_DQK), BF16), pltpu.VMEM((lt, M_HEADS * M_DQK), F32),
                        pltpu.VMEM((M_HEADS, M_DQK, 2 * M_DV), F32), pltpu.VMEM((8, 128), F32)],
        compiler_params=_cp("parallel", "arbitrary"),
        name="mlstm",
    )(p16, p16, p16, gates, gates_t, conv_w, norm_w)


def _rprep_kernel(*refs, tt, has_vres):
    if has_vres:
        (p_ref, pm_ref, vf_ref, mu_ref, mum_ref, wa_ref, wv_ref, wg_ref, hs_ref, wbias_ref, abias_ref, vbias_ref,
         kkw_ref, kaw_ref, rkw_ref,
         r_o, lw_o, k_o, v_o, kk_o, b_o, g_o, bv_o, carry_sc, carrym_sc) = refs
    else:
        (p_ref, pm_ref, mu_ref, mum_ref, wa_ref, wg_ref, hs_ref, wbias_ref, abias_ref,
         kkw_ref, kaw_ref, rkw_ref,
         r_o, lw_o, k_o, v_o, kk_o, b_o, g_o, bv_o, vown_o, carry_sc, carrym_sc) = refs
    ti = pl.program_id(1)

    @pl.when(ti == 0)
    def _():
        carry_sc[...] = jnp.zeros(carry_sc.shape, F32)
        carrym_sc[...] = jnp.zeros(carrym_sc.shape, F32)

    rows = lax.broadcasted_iota(jnp.int32, (tt, 512), 0)

    def shifted(ref, cref, mref, c0, width):
        x = ref[0, :, c0:c0 + width].astype(F32)
        prev = jnp.where(rows[:, :width] == 0, cref[0:1, c0:c0 + width], pltpu.roll(x, 1, 0))
        return x + mref[:, c0:c0 + width] * (prev - x)

    r = shifted(p_ref, carry_sc, mu_ref, 0, 512)
    k = shifted(p_ref, carry_sc, mu_ref, 512, 512)
    v = shifted(p_ref, carry_sc, mu_ref, 1024, 512)
    wa_in = shifted(pm_ref, carrym_sc, mum_ref, 0, 128)
    g_in = shifted(pm_ref, carrym_sc, mum_ref, 128, 128)
    lane = lax.broadcasted_iota(jnp.int32, (tt, 128), 1)
    wa_act = jnp.where(lane < 64, jnp.tanh(wa_in), wa_in).astype(BF16)
    wa = _dot(wa_act, wa_ref[...])
    w_log = -_softplus(-(wbias_ref[...] + wa[:, :512])) - 0.5
    lw_o[0] = -jnp.exp(w_log)
    a = _sigmoid(abias_ref[...] + wa[:, 512:])
    if has_vres:
        s_in = shifted(pm_ref, carrym_sc, mum_ref, 256, 128)
        mix = _sigmoid(vbias_ref[...] + _dot(s_in.astype(BF16), wv_ref[...]))
        v = v + (vf_ref[0].astype(F32) - v) * mix
    else:
        vown_o[0] = v.astype(vown_o.dtype)
    g_o[0] = _dot(_sigmoid(g_in).astype(BF16), wg_ref[...]).astype(g_o.dtype)
    hs = hs_ref[...]
    u = k * kkw_ref[...]
    nrm = jnp.sqrt(_dot_x2(u * u, hs))
    kk = u / jnp.maximum(nrm, 1e-12)
    k2 = k * (1.0 + (a - 1.0) * kaw_ref[...])
    bonus = _dot_x2(r * k2 * rkw_ref[...], hs)
    r_o[0] = r.astype(r_o.dtype)
    k_o[0] = k2.astype(k_o.dtype)
    v_o[0] = v.astype(v_o.dtype)
    kk_o[0] = kk.astype(kk_o.dtype)
    b_o[0] = (kk * a).astype(b_o.dtype)
    bv_o[0] = (bonus * v).astype(bv_o.dtype)
    carry_sc[0:1, :] = p_ref[0, tt - 1:tt, :].astype(F32)
    carrym_sc[0:1, :] = pm_ref[0, tt - 1:tt, :].astype(F32)


def _rwkv_prep(p16, v_first, mu, mum, w_wa, w_v, w_g, headsum, wbias, abias, vbias, kkw, kaw, rkw, tt):
    bsz, s, _ = p16.shape
    tt = min(tt, s)
    has_vres = v_first is not None
    full = lambda b, i: (0, 0)
    tok = lambda b, i: (b, i, 0)
    vec = pl.BlockSpec((1, 512), full)
    ins = [p16, p16]
    specs = [pl.BlockSpec((1, tt, 1536), lambda b, i: (b, i, C_RKV // 1536)),
             pl.BlockSpec((1, tt, 512), lambda b, i: (b, i, C_RMISC // 512))]
    if has_vres:
        ins.append(v_first)
        specs.append(pl.BlockSpec((1, tt, 512), tok))
    ins += [mu, mum, w_wa]
    specs += [pl.BlockSpec((1, 1536), full), pl.BlockSpec((1, 512), full), pl.BlockSpec((128, 1024), full)]
    if has_vres:
        ins.append(w_v)
        specs.append(pl.BlockSpec((128, 512), full))
    ins += [w_g, headsum, wbias, abias]
    specs += [pl.BlockSpec((128, 512), full), pl.BlockSpec((512, 512), full), vec, vec]
    if has_vres:
        ins.append(vbias)
        specs.append(vec)
    ins += [kkw, kaw, rkw]
    specs += [vec, vec, vec]
    n_out = 8 if has_vres else 9
    dts = [BF16, F32, BF16, BF16, BF16, BF16, BF16, BF16] + ([] if has_vres else [BF16])
    return pl.pallas_call(
        functools.partial(_rprep_kernel, tt=tt, has_vres=has_vres),
        grid=(bsz, s // tt),
        in_specs=specs,
        out_specs=[pl.BlockSpec((1, tt, 512), tok)] * n_out,
        out_shape=[jax.ShapeDtypeStruct((bsz, s, 512), dt) for dt in dts],
        scratch_shapes=[pltpu.VMEM((8, 1536), F32), pltpu.VMEM((8, 512), F32)],
        compiler_params=_cp("parallel", "arbitrary"),
        name="rwkv_prep",
    )(*ins)


def _rchunk_kernel(r_ref, lw_ref, k_ref, v_ref, kk_ref, b_ref, m_o, n_o, rq_o, y0_o, *, nck):
    L = R_CHUNK
    ri = lax.broadcasted_iota(jnp.int32, (L, L), 0)
    ci = lax.broadcasted_iota(jnp.int32, (L, L), 1)
    lower = (ci <= ri).astype(BF16)
    strict = ci < ri
    incl = ci <= ri
    eye = (ci == ri).astype(F32)

    kk_t, r_t, b_h, k_h, b_w, k_w, w_last, vf = [], [], [], [], [], [], [], []
    for c in range(nck):
        rs = slice(c * L, (c + 1) * L)
        lw = lw_ref[0, rs, :]
        l1, l2 = _split(lw)
        cl = _dot(lower, l1) + _dot(lower, l2)
        cl_last = cl[L - 1:L, :]
        w_inv = jnp.exp(-cl)
        w_rem = jnp.exp(cl_last - cl)
        kf = k_ref[0, rs, :].astype(F32)
        bf = b_ref[0, rs, :].astype(F32)
        kk_t.append((kk_ref[0, rs, :].astype(F32) * jnp.exp(cl - lw)).astype(BF16))
        r_t.append(r_ref[0, rs, :].astype(F32) * jnp.exp(cl))
        b_h.append((bf * w_inv).astype(BF16))
        k_h.append((kf * w_inv).astype(BF16))
        b_w.append((bf * w_rem).astype(BF16))
        k_w.append((kf * w_rem).astype(BF16))
        w_last.append(jnp.exp(cl_last))
        vf.append(v_ref[0, rs, :])

    J = [(c, slice(h * R_DH, (h + 1) * R_DH)) for c in range(nck) for h in range(R_HEADS)]
    N = range(len(J))
    a = [_dot_nt(jnp.concatenate([kk_t[c][:, sl], r_t[c][:, sl].astype(BF16)], axis=0),
                 jnp.concatenate([b_h[c][:, sl], k_h[c][:, sl]], axis=0)) for c, sl in J]
    a_ak = [jnp.where(strict, a[j][:L, L:], 0.0).astype(BF16) for j in N]
    a_rb = [jnp.where(incl, a[j][L:, :L], 0.0).astype(BF16) for j in N]
    a_rk = [jnp.where(incl, a[j][L:, L:], 0.0).astype(BF16) for j in N]
    x = [jnp.where(strict, -a[j][:L, :L], 0.0) for j in N]
    t = [eye + x[j] for j in N]
    for _ in range(5):
        x = [_dot16(x[j], x[j]) for j in N]
        t = [t[j] + _dot16(t[j], x[j]) for j in N]
    vh = [vf[c][:, sl] for c, sl in J]
    akv = [_dot(a_ak[j], vh[j]) for j in N]
    rkv = [_dot(a_rk[j], vh[j]) for j in N]
    pq16 = [_dot16(t[j], jnp.concatenate([kk_t[c][:, sl].astype(F32), akv[j]], axis=1)).astype(BF16)
            for j, (c, sl) in enumerate(J)]
    rq_y0 = [jnp.concatenate([r_t[c][:, sl], rkv[j]], axis=1) - _dot(a_rb[j], pq16[j]) for j, (c, sl) in enumerate(J)]
    pqb = [_dot_tn(pq16[j], b_w[c][:, sl]) for j, (c, sl) in enumerate(J)]
    vk = [_dot_tn(vh[j], k_w[c][:, sl]) for j, (c, sl) in enumerate(J)]
    for j, (c, sl) in enumerate(J):
        rs = slice(c * L, (c + 1) * L)
        rq_o[0, rs, sl] = rq_y0[j][:, :R_DH]
        y0_o[0, rs, sl] = rq_y0[j][:, R_DH:]
        h = j % R_HEADS
        m_o[0, c, h] = eye * w_last[c][:, sl] - pqb[j][:R_DH]
        n_o[0, c, h] = vk[j] - pqb[j][R_DH:]


def _rwkv_chunks(r, lw, k, v, kk, b, nck):
    bsz, s, w = r.shape
    L = R_CHUNK
    nc = s // L
    tok = lambda bi, c: (bi, c, 0)
    st = lambda bi, c: (bi, c, 0, 0, 0)
    sshape = (bsz, nc, R_HEADS, R_DH, R_DH)
    sblock = (1, nck, R_HEADS, R_DH, R_DH)
    return pl.pallas_call(
        functools.partial(_rchunk_kernel, nck=nck),
        grid=(bsz, nc // nck),
        in_specs=[pl.BlockSpec((1, nck * L, w), tok)] * 6,
        out_specs=[pl.BlockSpec(sblock, st), pl.BlockSpec(sblock, st),
                   pl.BlockSpec((1, nck * L, w), tok), pl.BlockSpec((1, nck * L, w), tok)],
        out_shape=[jax.ShapeDtypeStruct(sshape, F32), jax.ShapeDtypeStruct(sshape, F32),
                   jax.ShapeDtypeStruct((bsz, s, w), F32), jax.ShapeDtypeStruct((bsz, s, w), F32)],
        compiler_params=_cp("parallel", "parallel"),
        name="rwkv_chunks",
    )(r, lw, k, v, kk, b)


def _rstate_kernel(m_ref, n_ref, s_ref, s_sc, *, nck):
    H = range(R_HEADS)

    @pl.when(pl.program_id(1) == 0)
    def _():
        s_sc[...] = jnp.zeros(s_sc.shape, F32)

    for c in range(nck):
        st = [s_sc[h] for h in H]
        sp = [_split(st[h]) for h in H]
        mp = [_split(m_ref[0, c, h]) for h in H]
        snew = [_dot(sp[h][0], mp[h][0]) + (_dot(sp[h][1], mp[h][0]) + _dot(sp[h][0], mp[h][1])) for h in H]
        for h in H:
            s_ref[0, c, h] = st[h]
            s_sc[h] = snew[h] + n_ref[0, c, h]


def _rout_kernel(s_ref, rq_ref, y0_ref, bv_ref, g_ref, lng_ref, lnb_ref, o_ref, *, nck):
    L = R_CHUNK
    J = [(c, h) for c in range(nck) for h in range(R_HEADS)]
    rsl = lambda c: slice(c * L, (c + 1) * L)
    hsl = lambda h: slice(h * R_DH, (h + 1) * R_DH)
    sp = [_split(s_ref[0, c, h]) for c, h in J]
    rp = [_split(rq_ref[0, rsl(c), hsl(h)]) for c, h in J]
    y = [y0_ref[0, rsl(c), hsl(h)] + _dot_nt(rp[j][0], sp[j][0])
         + (_dot_nt(rp[j][1], sp[j][0]) + _dot_nt(rp[j][0], sp[j][1])) for j, (c, h) in enumerate(J)]
    for j, (c, h) in enumerate(J):
        rs, sl = rsl(c), hsl(h)
        mu = jnp.mean(y[j], axis=1, keepdims=True)
        yc = y[j] - mu
        var = jnp.mean(yc * yc, axis=1, keepdims=True)
        yn = yc * lax.rsqrt(var + R_GN_EPS) * lng_ref[:, sl] + lnb_ref[:, sl]
        o_ref[0, rs, sl] = ((yn + bv_ref[0, rs, sl].astype(F32)) * g_ref[0, rs, sl].astype(F32)).astype(o_ref.dtype)


def _rwkv_scan(m, n, rq, y0, bv, g, lng, lnb, nck_state, nck_out):
    bsz, s, w = rq.shape
    L = R_CHUNK
    nc = s // L
    tok = lambda bi, c: (bi, c, 0)
    st = lambda bi, c: (bi, c, 0, 0, 0)
    full = lambda bi, c: (0, 0)
    sblock = lambda k: (1, k, R_HEADS, R_DH, R_DH)
    states = pl.pallas_call(
        functools.partial(_rstate_kernel, nck=nck_state),
        grid=(bsz, nc // nck_state),
        in_specs=[pl.BlockSpec(sblock(nck_state), st), pl.BlockSpec(sblock(nck_state), st)],
        out_specs=pl.BlockSpec(sblock(nck_state), st),
        out_shape=jax.ShapeDtypeStruct(m.shape, F32),
        scratch_shapes=[pltpu.VMEM((R_HEADS, R_DH, R_DH), F32)],
        compiler_params=_cp("parallel", "arbitrary"),
        name="rwkv_state",
    )(m, n)
    k = nck_out
    return pl.pallas_call(
        functools.partial(_rout_kernel, nck=k),
        grid=(bsz, nc // k),
        in_specs=[pl.BlockSpec(sblock(k), st),
                  pl.BlockSpec((1, k * L, w), tok), pl.BlockSpec((1, k * L, w), tok),
                  pl.BlockSpec((1, k * L, w), tok), pl.BlockSpec((1, k * L, w), tok),
                  pl.BlockSpec((1, w), full), pl.BlockSpec((1, w), full)],
        out_specs=pl.BlockSpec((1, k * L, w), tok),
        out_shape=jax.ShapeDtypeStruct((bsz, s, w), BF16),
        compiler_params=_cp("parallel", "parallel"),
        name="rwkv_out",
    )(states, rq, y0, bv, g, lng, lnb)


def _merge_kernel(ha_ref, hb_ref, y_ref, gate_ref, x_ref, mup_ref, fup_ref, rup_ref, wo_ref, g_ref, b_ref,
                  o32_ref, o16_ref):
    d = D_MODEL
    merged = (_sigmoid(gate_ref[:, 0:d].astype(F32)) * _dot(ha_ref[...], mup_ref[...])
              + _sigmoid(gate_ref[:, d:2 * d].astype(F32)) * _dot(hb_ref[...], fup_ref[...])
              + _sigmoid(gate_ref[:, 2 * d:3 * d].astype(F32)) * _dot(y_ref[...], rup_ref[...]))
    mix = _dot(merged.astype(BF16), wo_ref[...])
    out = _ln(ALPHA * x_ref[...] + mix, g_ref[...], b_ref[...])
    o32_ref[...] = out
    o16_ref[...] = out.astype(BF16)


def _merge(ha, hb, y, p16, x32, mup, fup, rup, wo, g, b, tm):
    t, d = x32.shape
    tm = min(tm, t)
    full = lambda i: (0, 0)
    tok = lambda i: (i, 0)
    return pl.pallas_call(
        _merge_kernel,
        grid=(t // tm,),
        in_specs=[pl.BlockSpec((tm, 512), tok), pl.BlockSpec((tm, 512), tok), pl.BlockSpec((tm, 512), tok),
                  pl.BlockSpec((tm, 3 * d), lambda i: (i, C_GATE // (3 * d))),
                  pl.BlockSpec((tm, d), tok),
                  pl.BlockSpec((512, d), full), pl.BlockSpec((512, d), full), pl.BlockSpec((512, d), full),
                  pl.BlockSpec((d, d), full), pl.BlockSpec((1, d), full), pl.BlockSpec((1, d), full)],
        out_specs=[pl.BlockSpec((tm, d), tok), pl.BlockSpec((tm, d), tok)],
        out_shape=[jax.ShapeDtypeStruct((t, d), F32), jax.ShapeDtypeStruct((t, d), BF16)],
        compiler_params=_cp("parallel"),
        name="merge_ln1",
    )(ha, hb, y, p16, x32, mup, fup, rup, wo, g, b)


def _xattn_kernel(x16_ref, x32_ref, k_ref, v_ref, wq_ref, wo_ref, g_ref, b_ref, o32_ref, o16_ref):
    q = _dot(x16_ref[0], wq_ref[...]).astype(BF16)
    outs = []
    for h in range(X_HEADS):
        sl = slice(h * X_DH, (h + 1) * X_DH)
        lg = _dot_nt(q[:, sl], k_ref[0, :, sl]) * (X_DH ** -0.5)
        mx = jnp.max(lg, axis=1, keepdims=True)
        e = jnp.exp(lg - mx)
        p = e / jnp.sum(e, axis=1, keepdims=True)
        outs.append(_dot(p.astype(BF16), v_ref[0, :, sl]))
    o = jnp.concatenate(outs, axis=1).astype(BF16)
    out = _ln(ALPHA * x32_ref[0] + _dot(o, wo_ref[...]), g_ref[...], b_ref[...])
    o32_ref[0] = out
    o16_ref[0] = out.astype(BF16)


def _xattn(x16, x32, kk, vv, wq, wo, g, b, tm):
    bsz, s, d = x32.shape
    tm = min(tm, s)
    m = kk.shape[1]
    full = lambda bi, i: (0, 0)
    tok = lambda bi, i: (bi, i, 0)
    mem = lambda bi, i: (bi, 0, 0)
    return pl.pallas_call(
        _xattn_kernel,
        grid=(bsz, s // tm),
        in_specs=[pl.BlockSpec((1, tm, d), tok), pl.BlockSpec((1, tm, d), tok),
                  pl.BlockSpec((1, m, d), mem), pl.BlockSpec((1, m, d), mem),
                  pl.BlockSpec((d, d), full), pl.BlockSpec((d, d), full),
                  pl.BlockSpec((1, d), full), pl.BlockSpec((1, d), full)],
        out_specs=[pl.BlockSpec((1, tm, d), tok), pl.BlockSpec((1, tm, d), tok)],
        out_shape=[jax.ShapeDtypeStruct((bsz, s, d), F32), jax.ShapeDtypeStruct((bsz, s, d), BF16)],
        compiler_params=_cp("parallel", "parallel"),
        name="xattn_ln2",
    )(x16, x32, kk, vv, wq, wo, g, b)


def _ffn_kernel(x16_ref, x32_ref, wg_ref, wu_ref, wd_ref, g_ref, b_ref, o32_ref, o16_ref, acc_sc):
    j = pl.program_id(1)

    @pl.when(j == 0)
    def _():
        acc_sc[...] = jnp.zeros(acc_sc.shape, F32)

    x = x16_ref[...]
    act = _silu(_dot(x, wg_ref[...])) * _dot(x, wu_ref[...])
    acc_sc[...] += _dot(act.astype(BF16), wd_ref[...])

    @pl.when(j == pl.num_programs(1) - 1)
    def _():
        out = _ln(ALPHA * x32_ref[...] + acc_sc[...], g_ref[...], b_ref[...])
        o32_ref[...] = out
        o16_ref[...] = out.astype(BF16)


def _ffn(x16, x32, wgu, wd, g, b, tm, bf):
    t, d = x32.shape
    tm = min(tm, t)
    dff = wd.shape[0]
    nf = dff // bf
    tok = lambda i, j: (i, 0)
    full = lambda i, j: (0, 0)
    return pl.pallas_call(
        _ffn_kernel,
        grid=(t // tm, nf),
        in_specs=[pl.BlockSpec((tm, d), tok), pl.BlockSpec((tm, d), tok),
                  pl.BlockSpec((d, bf), lambda i, j: (0, j)),
                  pl.BlockSpec((d, bf), lambda i, j: (0, j + nf)),
                  pl.BlockSpec((bf, d), lambda i, j: (j, 0)),
                  pl.BlockSpec((1, d), full), pl.BlockSpec((1, d), full)],
        out_specs=[pl.BlockSpec((tm, d), tok), pl.BlockSpec((tm, d), tok)],
        out_shape=[jax.ShapeDtypeStruct((t, d), F32), jax.ShapeDtypeStruct((t, d), BF16)],
        scratch_shapes=[pltpu.VMEM((tm, d), F32)],
        compiler_params=_cp("parallel", "arbitrary"),
        name="ffn_ln3",
    )(x16, x32, wgu, wgu, wd, g, b)


RT_E1, RT_E2, RT_W1, RT_W2, RT_R1, RT_R2 = 0, 1, 2, 3, 4, 5


def _route_kernel(x_ref, rw_ref, rb_ref, info_ref, cnt_ref, carry_sc, *, tr):
    @pl.when(pl.program_id(0) == 0)
    def _():
        carry_sc[...] = jnp.zeros(carry_sc.shape, F32)

    lane = lax.broadcasted_iota(jnp.int32, (tr, 128), 1).astype(F32)
    logits = _dot_x3(x_ref[...], rw_ref[...]) + rb_ref[...]
    logits = jnp.where(lane < N_EXPERTS, logits, -jnp.inf)
    m1 = jnp.max(logits, axis=1, keepdims=True)
    i1 = jnp.min(jnp.where(logits == m1, lane, 128.0), axis=1, keepdims=True)
    rest = jnp.where(lane == i1, -jnp.inf, logits)
    m2 = jnp.max(rest, axis=1, keepdims=True)
    i2 = jnp.min(jnp.where(rest == m2, lane, 128.0), axis=1, keepdims=True)
    e2 = jnp.exp(m2 - m1)
    w1 = 1.0 / (1.0 + e2)
    w2 = e2 / (1.0 + e2)
    oh1 = lane == i1
    oh2 = lane == i2
    oh = jnp.logical_or(oh1, oh2).astype(F32)
    r = lax.broadcasted_iota(jnp.int32, (tr, tr), 0)
    c = lax.broadcasted_iota(jnp.int32, (tr, tr), 1)
    before = (c < r).astype(BF16)
    carry = carry_sc[0:1, :]
    rank = _dot(before, oh.astype(BF16)) + carry
    r1 = jnp.sum(jnp.where(oh1, rank, 0.0), axis=1, keepdims=True)
    r2 = jnp.sum(jnp.where(oh2, rank, 0.0), axis=1, keepdims=True)
    total = carry + jnp.sum(oh, axis=0, keepdims=True)
    carry_sc[...] = jnp.broadcast_to(total, carry_sc.shape)
    cnt_ref[...] = jnp.broadcast_to(total, cnt_ref.shape)
    rec = jnp.zeros((tr, 128), F32)
    for ln, val in ((RT_E1, i1), (RT_E2, i2), (RT_W1, w1), (RT_W2, w2), (RT_R1, r1), (RT_R2, r2)):
        rec = jnp.where(lane == float(ln), val, rec)
    info_ref[...] = rec


def _route(x32, rw, rb, tr):
    t, d = x32.shape
    tr = min(tr, t)
    full = lambda i: (0, 0)
    return pl.pallas_call(
        functools.partial(_route_kernel, tr=tr),
        grid=(t // tr,),
        in_specs=[pl.BlockSpec((tr, d), lambda i: (i, 0)), pl.BlockSpec((d, 128), full), pl.BlockSpec((1, 128), full)],
        out_specs=[pl.BlockSpec((tr, 128), lambda i: (i, 0)), pl.BlockSpec((8, 128), full)],
        out_shape=[jax.ShapeDtypeStruct((t, 128), F32), jax.ShapeDtypeStruct((8, 128), F32)],
        scratch_shapes=[pltpu.VMEM((8, 128), F32)],
        compiler_params=_cp("arbitrary"),
        name="moe_route",
    )(x32, rw, rb)


def _experts_kernel(te_ref, nu_ref, src_ref, srcn_ref, dst_ref, x_hbm, wg_ref, wu_ref, wd_ref, yinit_hbm, yg_hbm,
                    xbuf, x16_sc, acc_sc, ybuf, gsem, ssem, *, tm):
    del te_ref, yinit_hbm
    i = pl.program_id(0)
    j = pl.program_id(1)
    nf = pl.num_programs(1)
    nu = nu_ref[0]
    slot = i % 2

    def gather(idx_ref, s):
        return lambda r: pltpu.make_async_copy(x_hbm.at[pl.ds(idx_ref[0, 0, r], 1)], xbuf.at[s, pl.ds(r, 1)], gsem.at[s])

    def scatter(s):
        return lambda r: pltpu.make_async_copy(ybuf.at[s, pl.ds(r, 1)], yg_hbm.at[pl.ds(dst_ref[0, 0, r], 1)], ssem.at[s])

    def start_all(copy):
        lax.fori_loop(0, tm, lambda r, c: (copy(r).start(), c)[1], 0, unroll=8)

    def wait_gather(s):
        pltpu.make_async_copy(x_hbm.at[pl.ds(0, tm)], xbuf.at[s], gsem.at[s]).wait()

    def wait_scatter(s):
        pltpu.make_async_copy(ybuf.at[s], yg_hbm.at[pl.ds(0, tm)], ssem.at[s]).wait()

    @pl.when(i < nu)
    def _():
        @pl.when(j == 0)
        def _():
            @pl.when(i == 0)
            def _():
                start_all(gather(src_ref, slot))
            wait_gather(slot)
            x16_sc[...] = xbuf[slot].astype(BF16)
            acc_sc[...] = jnp.zeros(acc_sc.shape, F32)

        @pl.when(jnp.logical_and(j == 1, i + 1 < nu))
        def _():
            start_all(gather(srcn_ref, 1 - slot))

        x = x16_sc[...]
        act = _silu(_dot(x, wg_ref[0])) * _dot(x, wu_ref[0])
        acc_sc[...] += _dot(act.astype(BF16), wd_ref[0])

        @pl.when(j == nf - 1)
        def _():
            @pl.when(i > 0)
            def _():
                wait_scatter(1 - slot)
            ybuf[slot] = acc_sc[...]
            start_all(scatter(slot))

            @pl.when(i == nu - 1)
            def _():
                wait_scatter(slot)


def _experts(tile_expert, n_used, src_row, dst_row, x32, wgu, wd, tm, bf):
    rtot = src_row.shape[0]
    d = x32.shape[1]
    ne, dff, _ = wd.shape
    nf = dff // bf
    nt = rtot // tm
    assert nf >= 2
    ex = lambda i, te, nu: te[jnp.minimum(i, nu[0] - 1)]
    idx = lambda f: pl.BlockSpec((1, 1, tm), lambda i, j, te, nu: (f(i, nu), 0, 0), memory_space=pltpu.SMEM)
    cur = lambda i, nu: jnp.minimum(i, nu[0] - 1)
    nxt = lambda i, nu: jnp.minimum(i + 1, nu[0] - 1)
    src3 = src_row.reshape(nt, 1, tm)
    return pl.pallas_call(
        functools.partial(_experts_kernel, tm=tm),
        grid_spec=pltpu.PrefetchScalarGridSpec(
            num_scalar_prefetch=2,
            grid=(nt, nf),
            in_specs=[idx(cur), idx(nxt), idx(cur),
                      pl.BlockSpec(memory_space=pl.ANY),
                      pl.BlockSpec((1, d, bf), lambda i, j, te, nu: (ex(i, te, nu), 0, j)),
                      pl.BlockSpec((1, d, bf), lambda i, j, te, nu: (ex(i, te, nu), 0, j + nf)),
                      pl.BlockSpec((1, bf, d), lambda i, j, te, nu: (ex(i, te, nu), j, 0)),
                      pl.BlockSpec(memory_space=pl.ANY)],
            out_specs=pl.BlockSpec(memory_space=pl.ANY),
            scratch_shapes=[pltpu.VMEM((2, tm, d), F32), pltpu.VMEM((tm, d), BF16), pltpu.VMEM((tm, d), F32),
                            pltpu.VMEM((2, tm, d), F32), pltpu.SemaphoreType.DMA((2,)), pltpu.SemaphoreType.DMA((2,))]),
        out_shape=jax.ShapeDtypeStruct((rtot, d), F32),
        input_output_aliases={9: 0},
        compiler_params=_cp("arbitrary", "arbitrary"),
        name="moe_experts",
    )(tile_expert, n_used, src3, src3, dst_row.reshape(nt, 1, tm), x32, wgu, wgu, wd, jnp.zeros((rtot, d), F32))


def _combine_kernel(x_ref, y1_ref, y2_ref, info_ref, g_ref, b_ref, o_ref):
    info = info_ref[...]
    w1 = info[:, RT_W1:RT_W1 + 1]
    w2 = info[:, RT_W2:RT_W2 + 1]
    o_ref[...] = _ln(ALPHA * x_ref[...] + (w1 * y1_ref[...] + w2 * y2_ref[...]), g_ref[...], b_ref[...])


def _combine(x32, yg, info, g, b, tm):
    t, d = x32.shape
    tm = min(tm, t)
    nt = t // tm
    tok = lambda i: (i, 0)
    full = lambda i: (0, 0)
    return pl.pallas_call(
        _combine_kernel,
        grid=(nt,),
        in_specs=[pl.BlockSpec((tm, d), tok), pl.BlockSpec((tm, d), tok), pl.BlockSpec((tm, d), lambda i: (i + nt, 0)),
                  pl.BlockSpec((tm, 128), tok), pl.BlockSpec((1, d), full), pl.BlockSpec((1, d), full)],
        out_specs=pl.BlockSpec((tm, d), tok),
        out_shape=jax.ShapeDtypeStruct((t, d), F32),
        compiler_params=_cp("parallel"),
        name="moe_combine_ln3",
    )(x32, yg, yg, info, g, b)


def _moe(x32, rw, rb, wgu, wd, g, b, tm, bf):
    t, d = x32.shape
    tm = min(tm, t)
    info, cnt = _route(x32, rw, rb, 512)
    i32 = jnp.int32
    e1, e2 = info[:, RT_E1].astype(i32), info[:, RT_E2].astype(i32)
    r1, r2 = info[:, RT_R1].astype(i32), info[:, RT_R2].astype(i32)
    tiles_e = (cnt[0, :N_EXPERTS].astype(i32) + tm - 1) // tm
    tile_end = jnp.cumsum(tiles_e)
    start = (tile_end - tiles_e) * tm
    nt = (2 * t) // tm + N_EXPERTS
    n_used = tile_end[N_EXPERTS - 1:]
    tile_expert = jnp.minimum(jnp.searchsorted(tile_end, jnp.arange(nt, dtype=i32), side='right'),
                              N_EXPERTS - 1).astype(i32)
    pos = jnp.concatenate([start[e1] + r1, start[e2] + r2])
    rows = nt * tm
    slot_row = jnp.full((rows,), -1, i32).at[pos].set(jnp.arange(2 * t, dtype=i32))
    pad = slot_row < 0
    src_row = jnp.where(pad, 0, slot_row % t)
    dst_row = jnp.where(pad, 2 * t + jnp.cumsum(pad.astype(i32)) - 1, slot_row)
    yg = _experts(tile_expert, n_used, src_row, dst_row, x32, wgu, wd, tm, bf)
    return _combine(x32, yg, info, g, b, tm)


def _src_layout(with_vres):
    cols = [('m_qk', 512), ('m_v', 512), ('m_o', 512), ('m_i', 4), ('m_f', 4),
            ('f_q', 512), ('f_k', 512), ('f_v', 512), ('f_f', 8), ('gate', 3072),
            ('r_r', 512), ('r_k', 512), ('r_v', 512), ('r_w', 64), ('r_a', 64), ('r_g', 128)]
    if with_vres:
        cols.append(('r_vres', 32))
    layout, start = {}, 0
    for name, width in cols:
        layout[name] = (start, start + width)
        start += width
    return layout


def _relayout_cols(w, with_vres):
    lay = _src_layout(with_vres)
    lead = w.shape[:-1]
    z = lambda n: jnp.zeros(lead + (n,), w.dtype)
    c = lambda name: w[..., lay[name][0]:lay[name][1]]

    def slots(x, scale):
        x = x.reshape(lead + (F_HEADS, F_DH)) * scale
        return jnp.concatenate([x, jnp.zeros_like(x)], axis=-1).reshape(lead + (F_HEADS * F_PAD,))

    parts = [c('gate'), c('r_r'), c('r_k'), c('r_v'),
             c('r_w'), c('r_a'), c('r_g'),
             c('m_i'), c('m_f'), c('f_f'), z(16), c('r_vres') if with_vres else z(32), z(64), z(128),
             slots(c('f_q'), F_DH ** -0.5 * LOG2E), slots(c('f_k'), 1.0),
             c('m_qk'), c('m_v'), c('m_o'), c('f_v')]
    return jnp.concatenate(parts, axis=-1)


def _row(v):
    return v.reshape(1, -1).astype(F32)


def _branches(x16, bsz, s, prm, v_first):
    (w_in, b_in, m_conv, m_norm, m_up, f_up, r_mu, r_wbias, r_wB, r_abias, r_aB, r_vbias, r_vB,
     r_gB, r_kk, r_ka, r_rk, r_ln_g, r_ln_b, r_up, w_out, ln1_g, ln1_b) = prm
    with_vres = r_vbias is not None
    w_all = _relayout_cols(w_in, with_vres)
    b_all = _relayout_cols(b_in, with_vres).reshape(1, NP)
    p16 = _matmul_bias(x16, w_all.astype(BF16), b_all, BF16, 1024, 1024, "in_proj")
    gates = _matmul_bias(x16, w_all[:, C_SMALL:C_SMALL + 128].astype(BF16), b_all[:, C_SMALL:C_SMALL + 128],
                         F32, 2048, 128, "in_proj_gates")
    p16 = p16.reshape(bsz, s, NP)
    gates = gates.reshape(bsz, s, 128)
    gates_t = jnp.transpose(gates, (0, 2, 1))

    h_a = _mlstm(p16, gates, gates_t, m_conv.astype(F32), _row(m_norm), 256)
    q_aug, k_aug, v_aug = _fox_prep(p16, gates_t, 512)
    h_b = _fox_attention(q_aug, k_aug, v_aug, 512)
    lay = _src_layout(with_vres)
    r0 = lay['r_r'][0]
    mu = r_mu[:1536].reshape(1, 1536)
    mum = jnp.concatenate([r_mu[1536:1792], jnp.zeros((32,), F32),
                           r_mu[lay['r_vres'][0] - r0:] if with_vres else jnp.zeros((32,), F32),
                           jnp.zeros((192,), F32)]).reshape(1, 512)
    z = jnp.zeros((64, 512), F32)
    w_wa = jnp.concatenate([jnp.concatenate([r_wB, z], axis=1), jnp.concatenate([z, r_aB], axis=1)], axis=0).astype(BF16)
    w_v = None
    if with_vres:
        w_v = jnp.concatenate([jnp.zeros((32, 512), F32), r_vB, jnp.zeros((64, 512), F32)], axis=0).astype(BF16)
    hid = jnp.arange(512) // R_DH
    headsum = (hid[:, None] == hid[None, :]).astype(BF16)
    outs = _rwkv_prep(p16, v_first, mu, mum, w_wa, w_v, r_gB.astype(BF16), headsum, _row(r_wbias), _row(r_abias),
                      _row(r_vbias) if with_vres else None, _row(r_kk), _row(r_ka), _row(r_rk), 256)
    r, lw, k2, v, kk, b, g, bv = outs[:8]
    v_own = v_first if with_vres else outs[8]
    m_c, n_c, rq, y0 = _rwkv_chunks(r, lw, k2, v, kk, b, 4)
    y = _rwkv_scan(m_c, n_c, rq, y0, bv, g, _row(r_ln_g), _row(r_ln_b), 16, 2)
    return p16, h_a, h_b, y, v_own


def _token_mixer(x16, x32, bsz, s, prm, v_first):
    m_up, f_up, r_up, w_out, ln1_g, ln1_b = prm[4], prm[5], prm[19], prm[20], prm[21], prm[22]
    t = bsz * s
    p16, h_a, h_b, y, v_own = _branches(x16, bsz, s, prm, v_first)
    x1_32, x1_16 = _merge(h_a.reshape(t, 512), h_b.reshape(t, 512), y.reshape(t, 512), p16.reshape(t, NP), x32,
                          m_up.astype(BF16), f_up.astype(BF16), r_up.astype(BF16), w_out.astype(BF16),
                          _row(ln1_g), _row(ln1_b), 512)
    return x1_32, x1_16, v_own


def kernel(x, mem, mem_ln_g, mem_ln_b, w_in_0, b_in_0, m_conv_0, m_norm_0, m_up_0, f_up_0, r_mu_0, r_wbias_0, r_wB_0, r_abias_0, r_aB_0, r_gB_0, r_kk_0, r_ka_0, r_rk_0, r_ln_g_0, r_ln_b_0, r_up_0, w_out_0, ln1_g_0, ln1_b_0, x_wq_0, x_wkv_0, x_wo_0, ln2_g_0, ln2_b_0, ff_wgu_0, ff_wd_0, ln3_g_0, ln3_b_0, w_in_1, b_in_1, m_conv_1, m_norm_1, m_up_1, f_up_1, r_mu_1, r_wbias_1, r_wB_1, r_abias_1, r_aB_1, r_vbias_1, r_vB_1, r_gB_1, r_kk_1, r_ka_1, r_rk_1, r_ln_g_1, r_ln_b_1, r_up_1, w_out_1, ln1_g_1, ln1_b_1, x_wq_1, x_wkv_1, x_wo_1, ln2_g_1, ln2_b_1, ex_router_1, ex_router_b_1, ex_wgu_1, ex_wd_1, ln3_g_1, ln3_b_1):
    bsz, s, d = x.shape
    t = bsz * s
    x32 = x.reshape(t, d)
    x16 = x32.astype(BF16)

    mixers = (
        (w_in_0, b_in_0, m_conv_0, m_norm_0, m_up_0, f_up_0, r_mu_0, r_wbias_0, r_wB_0, r_abias_0, r_aB_0, None, None,
         r_gB_0, r_kk_0, r_ka_0, r_rk_0, r_ln_g_0, r_ln_b_0, r_up_0, w_out_0, ln1_g_0, ln1_b_0),
        (w_in_1, b_in_1, m_conv_1, m_norm_1, m_up_1, f_up_1, r_mu_1, r_wbias_1, r_wB_1, r_abias_1, r_aB_1, r_vbias_1, r_vB_1,
         r_gB_1, r_kk_1, r_ka_1, r_rk_1, r_ln_g_1, r_ln_b_1, r_up_1, w_out_1, ln1_g_1, ln1_b_1),
    )
    xattn = ((x_wq_0, x_wo_0, ln2_g_0, ln2_b_0), (x_wq_1, x_wo_1, ln2_g_1, ln2_b_1))
    kv0, kv1 = _mem_kv(mem, _row(mem_ln_g), _row(mem_ln_b), x_wkv_0.astype(BF16), x_wkv_1.astype(BF16))
    kvs = (kv0, kv1)

    v_first = None
    for l in range(DEPTH):
        x32, x16, v_own = _token_mixer(x16, x32, bsz, s, mixers[l], v_first)
        if l == 0:
            v_first = v_own
        wq, wo, g2, b2 = xattn[l]
        kv = kvs[l]
        x32, x16 = _xattn(x16.reshape(bsz, s, d), x32.reshape(bsz, s, d), kv[:, :, :d], kv[:, :, d:],
                          wq.astype(BF16), wo.astype(BF16), _row(g2), _row(b2), 512)
        x32, x16 = x32.reshape(t, d), x16.reshape(t, d)
        if l % 2 == 0:
            x32, x16 = _ffn(x16, x32, ff_wgu_0.astype(BF16), ff_wd_0.astype(BF16), _row(ln3_g_0), _row(ln3_b_0), 512, 1408)
        else:
            rw = jnp.concatenate([ex_router_1, jnp.zeros((d, 128 - N_EXPERTS), F32)], axis=1)
            rb = jnp.concatenate([ex_router_b_1, jnp.zeros((128 - N_EXPERTS,), F32)]).reshape(1, 128)
            x32 = _moe(x32, rw, rb, ex_wgu_1.astype(BF16), ex_wd_1.astype(BF16), _row(ln3_g_1), _row(ln3_b_1), 1024, 896)
    return x32.reshape(bsz, s, d)
```
